```python
import jax
import jax.numpy as jnp
from jax import lax
import numpy as np

D_MODEL = 1024
BATCH = 8
SEQ = 4096
DEPTH = 4

MLA_HEADS = 4
MLA_Q_LORA = 256
MLA_KV_LORA = 128
MLA_NOPE = 64
MLA_ROPE = 32
MLA_V = 64
ROPE_THETA = 10000.0
LRU_WIDTH = 256
LRU_BLOCKS = 4
LRU_BLOCK_W = LRU_WIDTH // LRU_BLOCKS
CONV_WIDTH = 4
LRU_C = 8.0
SB_HEADS = 4
SB_HD = 64
HG_HEADS = 4
HG_DK = 64
HG_DV = 64
HG_CHUNK = 64
Q_BLOCK = 128
GROUP_W = MLA_HEADS * MLA_V
D_MIX = MLA_HEADS * MLA_V + LRU_WIDTH + SB_HEADS * SB_HD + HG_HEADS * HG_DV
FFN_HIDDEN = -(-8 * D_MODEL // (3 * 256)) * 256
NORM_EPS = 1e-6
MASK_VALUE = -1e30
LB_FLOOR = 1e-30
IN_WIDTHS = (MLA_Q_LORA, MLA_KV_LORA, MLA_ROPE,
             LRU_WIDTH, LRU_WIDTH,
             SB_HEADS * SB_HD, SB_HEADS * SB_HD, SB_HEADS * SB_HD,
             HG_HEADS * HG_DK, HG_HEADS * HG_DK, HG_HEADS * HG_DV, HG_HEADS * HG_DV)
D_IN = sum(IN_WIDTHS)

kernel_name = 'hybrid_parallel_mla_rglru_stickbreak_hgrn2'


def rmsnorm(x, g):
    xf = x.astype(jnp.float32)
    y = xf * lax.rsqrt(jnp.mean(xf * xf, axis=-1, keepdims=True) + NORM_EPS)
    return (y * g.astype(jnp.float32)).astype(x.dtype)


def rope(x, positions):
    half = x.shape[-1] // 2
    inv_freq = ROPE_THETA ** (-jnp.arange(half, dtype=jnp.float32) / half)
    ang = positions.astype(jnp.float32)[..., None] * inv_freq
    ang = ang.reshape(ang.shape[:2] + (1,) * (x.ndim - 3) + (half,))
    cos, sin = jnp.cos(ang), jnp.sin(ang)
    xf = x.astype(jnp.float32)
    x1, x2 = xf[..., :half], xf[..., half:]
    return jnp.concatenate([x1 * cos - x2 * sin, x2 * cos + x1 * sin], axis=-1).astype(x.dtype)


def to_q_blocks(t):
    b, h, s, d = t.shape
    return t.reshape(b, h, s // Q_BLOCK, Q_BLOCK, d).transpose(2, 0, 1, 3, 4)


def from_q_blocks(t):
    n, b, h, qb, d = t.shape
    return t.transpose(1, 2, 0, 3, 4).reshape(b, h, n * qb, d)


def causal_softmax_attention(q, k, v, scale):
    s_len = q.shape[2]
    kpos = jnp.arange(s_len)

    def block(args):
        qb, start = args
        s = jnp.einsum('bhqd,bhkd->bhqk', qb, k).astype(jnp.float32) * scale
        qpos = start + jnp.arange(Q_BLOCK)
        s = jnp.where(kpos[None, :] <= qpos[:, None], s, MASK_VALUE)
        p = jax.nn.softmax(s, axis=-1).astype(v.dtype)
        return jnp.einsum('bhqk,bhkd->bhqd', p, v)

    starts = jnp.arange(s_len // Q_BLOCK) * Q_BLOCK
    return from_q_blocks(lax.map(block, (to_q_blocks(q), starts)))


def stick_breaking_attention(q, k, v):
    s_len = q.shape[2]
    kpos = jnp.arange(s_len)

    def block(args):
        qb, start = args
        z = jnp.einsum('bhqd,bhkd->bhqk', qb, k).astype(jnp.float32) * (SB_HD ** -0.5)
        qpos = start + jnp.arange(Q_BLOCK)
        strict = kpos[None, :] < qpos[:, None]
        log_beta = jax.nn.log_sigmoid(z)
        log_keep = jnp.where(strict, jax.nn.log_sigmoid(-z), 0.0)
        later = lax.cumsum(log_keep, axis=3, reverse=True) - log_keep
        w = jnp.where(strict, jnp.exp(log_beta + later), 0.0)
        return jnp.einsum('bhqk,bhkd->bhqd', w.astype(v.dtype), v)

    starts = jnp.arange(s_len // Q_BLOCK) * Q_BLOCK
    return from_q_blocks(lax.map(block, (to_q_blocks(q), starts)))


def mla_mixer(c_q, c_kv, k_rope, positions, q_norm_g, w_uq, kv_norm_g, w_ukv):
    b, s, _ = c_q.shape
    q = (rmsnorm(c_q, q_norm_g) @ w_uq).reshape(b, s, MLA_HEADS, MLA_NOPE + MLA_ROPE)
    q_nope, q_pe = q[..., :MLA_NOPE], rope(q[..., MLA_NOPE:], positions)
    kv = (rmsnorm(c_kv, kv_norm_g) @ w_ukv).reshape(b, s, MLA_HEADS, MLA_NOPE + MLA_V)
    k_nope, v = kv[..., :MLA_NOPE], kv[..., MLA_NOPE:]
    k_pe = rope(k_rope, positions)
    qh = jnp.concatenate([q_nope, q_pe], axis=-1)
    kh = jnp.concatenate([k_nope, jnp.broadcast_to(k_pe[:, :, None, :], (b, s, MLA_HEADS, MLA_ROPE))], axis=-1)
    out = causal_softmax_attention(qh.transpose(0, 2, 1, 3), kh.transpose(0, 2, 1, 3),
                                   v.transpose(0, 2, 1, 3), (MLA_NOPE + MLA_ROPE) ** -0.5)
    return out.transpose(0, 2, 1, 3).reshape(b, s, MLA_HEADS * MLA_V)


def rglru_mixer(xb, gate, conv_w, conv_b, w_a, b_a, w_x, b_x, lam):
    b, s, w = xb.shape
    xc = lax.conv_general_dilated(xb, conv_w[:, None, :], window_strides=(1,),
                                  padding=((CONV_WIDTH - 1, 0),),
                                  dimension_numbers=('NWC', 'WIO', 'NWC'),
                                  feature_group_count=w) + conv_b
    xblk = xc.reshape(b, s, LRU_BLOCKS, LRU_BLOCK_W)
    rec_gate = jax.nn.sigmoid(jnp.einsum('bsnc,ncd->bsnd', xblk, w_a).reshape(b, s, w) + b_a)
    in_gate = jax.nn.sigmoid(jnp.einsum('bsnc,ncd->bsnd', xblk, w_x).reshape(b, s, w) + b_x)
    log_a = -LRU_C * rec_gate.astype(jnp.float32) * jax.nn.softplus(-lam.astype(jnp.float32))
    a = jnp.exp(log_a)
    u = jnp.sqrt(jnp.maximum(-jnp.expm1(2.0 * log_a), 0.0)) * (in_gate * xc).astype(jnp.float32)

    def combine(left, right):
        a_l, b_l = left
        a_r, b_r = right
        return a_l * a_r, a_r * b_l + b_r

    _, h = lax.associative_scan(combine, (a, u), axis=1)
    return h.astype(xb.dtype) * jax.nn.gelu(gate)


def chunked_gated_recurrence(q, k, v, log_f):
    b, s, h, dk = q.shape
    dv = v.shape[-1]
    n = s // HG_CHUNK

    def chunks(t):
        return t.reshape(b, n, HG_CHUNK, h, t.shape[-1]).transpose(1, 0, 3, 2, 4)

    incl = jnp.tril(jnp.ones((HG_CHUNK, HG_CHUNK), dtype=bool))[:, :, None]

    def step(state, inp):
        qc, kc, vc, gc = inp
        cum = jnp.cumsum(gc, axis=2)
        diff = cum[:, :, :, None, :] - cum[:, :, None, :, :]
        decay = jnp.where(incl, jnp.exp(jnp.where(incl, diff, 0.0)), 0.0)
        scores = jnp.einsum('bhtd,bhsd,bhtsd->bhts', qc, kc, decay)
        o = (jnp.einsum('bhts,bhse->bhte', scores, vc)
             + jnp.einsum('bhtd,bhde->bhte', qc * jnp.exp(cum), state))
        last = cum[:, :, -1, :]
        state = (jnp.exp(last)[..., None] * state
                 + jnp.einsum('bhsd,bhse->bhde', kc * jnp.exp(last[:, :, None, :] - cum), vc))
        return state, o

    state0 = jnp.zeros((b, h, dk, dv), jnp.float32)
    _, o = lax.scan(step, state0, (chunks(q), chunks(k), chunks(v), chunks(log_f)))
    return o.transpose(1, 0, 3, 2, 4).reshape(b, s, h, dv)


def hgrn2_mixer(q, f_logit, i, g, lb, norm_g):
    b, s, _ = q.shape
    lb = lb.astype(jnp.float32)
    fz = f_logit.astype(jnp.float32)
    log_f = jnp.logaddexp(jnp.log(jnp.maximum(lb, LB_FLOOR)), jnp.log1p(-lb) + jax.nn.log_sigmoid(fz))
    key = (1.0 - lb) * jax.nn.sigmoid(-fz)
    heads_k = lambda t: t.reshape(b, s, HG_HEADS, HG_DK)
    o = chunked_gated_recurrence(heads_k(q.astype(jnp.float32)), heads_k(key),
                                 i.astype(jnp.float32).reshape(b, s, HG_HEADS, HG_DV), heads_k(log_f))
    o = rmsnorm(o, norm_g.reshape(HG_HEADS, HG_DV)).reshape(b, s, HG_HEADS * HG_DV)
    return (o * jax.nn.silu(g.astype(jnp.float32))).astype(q.dtype)


def setup_inputs(seed: int = 0) -> dict:
    key = jax.random.key(seed)
    ks = jax.random.split(key, 26)
    nrm = lambda k, shape, scale: jax.random.normal(k, shape, jnp.float32) * scale
    gain = lambda k, shape: 1.0 + 0.05 * jax.random.normal(k, shape, jnp.float32)
    x = jax.random.normal(ks[0], (BATCH, SEQ, D_MODEL), jnp.float32)
    offset = jax.random.randint(ks[1], (BATCH, 1), 0, SEQ, dtype=jnp.int32)
    positions = (offset + jnp.arange(SEQ, dtype=jnp.int32)[None, :]).astype(jnp.int32)
    u = jax.random.uniform(ks[13], (DEPTH, LRU_WIDTH), jnp.float32, minval=0.9, maxval=0.999)
    a0 = u ** (1.0 / LRU_C)
    lru_lambda = jnp.log(a0) - jnp.log1p(-a0)
    return {
        'x': x,
        'positions': positions,
        'ln1_g': gain(ks[2], (DEPTH, D_MODEL)),
        'w_in': nrm(ks[3], (DEPTH, D_MODEL, D_IN), D_MODEL ** -0.5),
        'mla_q_norm_g': gain(ks[4], (DEPTH, MLA_Q_LORA)),
        'mla_w_uq': nrm(ks[5], (DEPTH, MLA_Q_LORA, MLA_HEADS * (MLA_NOPE + MLA_ROPE)), MLA_Q_LORA ** -0.5),
        'mla_kv_norm_g': gain(ks[6], (DEPTH, MLA_KV_LORA)),
        'mla_w_ukv': nrm(ks[7], (DEPTH, MLA_KV_LORA, MLA_HEADS * (MLA_NOPE + MLA_V)), MLA_KV_LORA ** -0.5),
        'lru_conv_w': nrm(ks[8], (DEPTH, CONV_WIDTH, LRU_WIDTH), CONV_WIDTH ** -0.5),
        'lru_conv_b': nrm(ks[9], (DEPTH, LRU_WIDTH), 0.01),
        'lru_w_a': nrm(ks[10], (DEPTH, LRU_BLOCKS, LRU_BLOCK_W, LRU_BLOCK_W), LRU_BLOCK_W ** -0.5),
        'lru_b_a': nrm(ks[11], (DEPTH, LRU_WIDTH), 0.01),
        'lru_w_x': nrm(ks[12], (DEPTH, LRU_BLOCKS, LRU_BLOCK_W, LRU_BLOCK_W), LRU_BLOCK_W ** -0.5),
        'lru_b_x': nrm(ks[14], (DEPTH, LRU_WIDTH), 0.01),
        'lru_lambda': lru_lambda,
        'hgrn_lb_logits': nrm(ks[15], (DEPTH, HG_HEADS * HG_DK), 1.0),
        'hgrn_norm_g': gain(ks[16], (DEPTH, HG_HEADS * HG_DV)),
        'group_norm_g': gain(ks[17], (DEPTH, 3, GROUP_W)),
        'w_out': nrm(ks[18], (DEPTH, D_MIX, D_MODEL), D_MIX ** -0.5),
        'ln2_g': gain(ks[19], (DEPTH, D_MODEL)),
        'w_ffn_gate': nrm(ks[20], (DEPTH, D_MODEL, FFN_HIDDEN), D_MODEL ** -0.5),
        'w_ffn_up': nrm(ks[21], (DEPTH, D_MODEL, FFN_HIDDEN), D_MODEL ** -0.5),
        'w_ffn_down': nrm(ks[22], (DEPTH, FFN_HIDDEN, D_MODEL), FFN_HIDDEN ** -0.5),
        'final_norm_g': gain(ks[23], (D_MODEL,)),
    }


def reference(x, positions, ln1_g, w_in, mla_q_norm_g, mla_w_uq, mla_kv_norm_g, mla_w_ukv,
              lru_conv_w, lru_conv_b, lru_w_a, lru_b_a, lru_w_x, lru_b_x, lru_lambda,
              hgrn_lb_logits, hgrn_norm_g, group_norm_g, w_out, ln2_g,
              w_ffn_gate, w_ffn_up, w_ffn_down, final_norm_g):
    b, s, _ = x.shape
    split_points = np.cumsum(IN_WIDTHS)[:-1].tolist()
    lb_p = jax.nn.softmax(hgrn_lb_logits.astype(jnp.float32), axis=0)
    lower_bounds = jnp.cumsum(lb_p, axis=0) - lb_p[0]
    h = x
    for l in range(DEPTH):
        u = rmsnorm(h, ln1_g[l])
        proj = u @ w_in[l]
        (c_q, c_kv, k_rope, lru_x, lru_gate, sb_q, sb_k, sb_v,
         hg_q, hg_f, hg_i, hg_g) = jnp.split(proj, split_points, axis=-1)
        y_a = mla_mixer(c_q, c_kv, k_rope, positions, mla_q_norm_g[l], mla_w_uq[l],
                        mla_kv_norm_g[l], mla_w_ukv[l])
        y_b = rglru_mixer(lru_x, lru_gate, lru_conv_w[l], lru_conv_b[l], lru_w_a[l], lru_b_a[l],
                          lru_w_x[l], lru_b_x[l], lru_lambda[l])
        sb_heads = lambda t: t.reshape(b, s, SB_HEADS, SB_HD).transpose(0, 2, 1, 3)
        y_c = stick_breaking_attention(sb_heads(sb_q), sb_heads(sb_k), sb_heads(sb_v))
        y_c = y_c.transpose(0, 2, 1, 3).reshape(b, s, SB_HEADS * SB_HD)
        y_d = hgrn2_mixer(hg_q, hg_f, hg_i, hg_g, lower_bounds[l], hgrn_norm_g[l])
        mixed = jnp.concatenate([rmsnorm(y_a, group_norm_g[l, 0]),
                                 rmsnorm(y_b, group_norm_g[l, 1]),
                                 rmsnorm(y_c, group_norm_g[l, 2]),
                                 y_d], axis=-1)
        h = h + mixed @ w_out[l]
        u = rmsnorm(h, ln2_g[l])
        h = h + (jax.nn.silu(u @ w_ffn_gate[l]) * (u @ w_ffn_up[l])) @ w_ffn_down[l]
    return rmsnorm(h, final_norm_g)
```

```python
import functools
import math

import numpy as np
import jax
import jax.numpy as jnp
from jax import lax
from jax.experimental import pallas as pl
from jax.experimental.pallas import tpu as pltpu

F32 = jnp.float32
BF16 = jnp.bfloat16

D_MODEL = 1024
N_HEADS = 4
MLA_Q_LORA = 256
MLA_KV_LORA = 128
MLA_NOPE = 64
MLA_ROPE = 32
MLA_V = 64
ROPE_THETA = 10000.0
LRU_WIDTH = 256
CONV_WIDTH = 4
LRU_C = 8.0
HEAD_DIM = 64
GROUP_W = 256
FFN_HIDDEN = 2816
NORM_EPS = 1e-6
MASK_VALUE = -1e30
LB_FLOOR = 1e-30

LANES = 128
VMEM_LIMIT_BYTES = 56 * 1024 * 1024

ROW_TILE = 512
ATTN_TILE = 256
LRU_TIME_TILE = 256
HG_CHUNK = 128
HG_LEVELS = int(math.log2(HG_CHUNK))
FFN_CHUNK = 1408

IN_MLA = (0, 512)
IN_LRU = (512, 1024)
IN_SB = (1024, 1792)
IN_HG = (1792, 2816)
D_IN_PAD = 2816


def _params(*semantics):
    return pltpu.CompilerParams(dimension_semantics=semantics, vmem_limit_bytes=VMEM_LIMIT_BYTES)


def _rms(x, g):
    return x * lax.rsqrt(jnp.mean(x * x, axis=-1, keepdims=True) + NORM_EPS) * g


def _dot(a, b):
    return jnp.dot(a, b, preferred_element_type=F32)


def _dot_nt(a, b):
    return lax.dot_general(a, b, (((1,), (1,)), ((), ())), preferred_element_type=F32)


def _dot_tn(a, b):
    return lax.dot_general(a, b, (((0,), (0,)), ((), ())), preferred_element_type=F32)


def _split_bf16(x):
    hi = x.astype(BF16)
    lo = (x - hi.astype(F32)).astype(BF16)
    return hi, lo


def _sigmoid(x):
    return 1.0 / (1.0 + jnp.exp(-x))


def _log_sigmoid(x):
    return jnp.minimum(x, 0.0) - jnp.log1p(jnp.exp(-jnp.abs(x)))


def _rope_table_kernel(pos_ref, freq_ref, m1_ref, m2_ref, c_ref, s1_ref, s2_ref):
    ang = pos_ref[...] * freq_ref[...]
    sn = jnp.sin(ang)
    c_ref[...] = jnp.cos(ang)
    s1_ref[...] = sn * m1_ref[...]
    s2_ref[...] = sn * m2_ref[...]


def _rope_tables(positions):
    m = positions.size
    half = MLA_ROPE // 2
    inv_freq = ROPE_THETA ** (-jnp.arange(half, dtype=F32) / half)
    zeros = jnp.zeros((MLA_NOPE,), F32)
    tail = jnp.zeros((LANES - MLA_NOPE - MLA_ROPE,), F32)
    freq = jnp.concatenate([zeros, inv_freq, inv_freq, tail]).reshape(1, LANES)
    ones = jnp.ones((half,), F32)
    m1 = jnp.concatenate([zeros, -ones, 0 * ones, tail]).reshape(1, LANES)
    m2 = jnp.concatenate([zeros, 0 * ones, ones, tail]).reshape(1, LANES)
    pos = positions.astype(F32).reshape(m, 1)
    tm = ROW_TILE
    row = pl.BlockSpec((1, LANES), lambda i: (0, 0))
    tab = pl.BlockSpec((tm, LANES), lambda i: (i, 0))
    shape = jax.ShapeDtypeStruct((m, LANES), F32)
    return pl.pallas_call(
        _rope_table_kernel,
        out_shape=(shape, shape, shape),
        grid=(m // tm,),
        in_specs=[pl.BlockSpec((tm, 1), lambda i: (i, 0)), row, row, row],
        out_specs=(tab, tab, tab),
        compiler_params=_params("arbitrary"),
        name="rope_tables",
    )(pos, freq, m1, m2)


def _in_proj_kernel(h_ref, g1_ref, win_ref, qg_ref, wuq_ref, kvg_ref, wukv_ref,
                    c_ref, s1_ref, s2_ref,
                    qm_ref, km_ref, vm_ref, lru_ref, sb_ref, hg_ref):
    u = _rms(h_ref[...], g1_ref[...]).astype(BF16)
    lru_ref[...] = _dot(u, win_ref[:, IN_LRU[0]:IN_LRU[1]])
    sb_ref[...] = _dot(u, win_ref[:, IN_SB[0]:IN_SB[1]]).astype(BF16)
    hg_ref[...] = _dot(u, win_ref[:, IN_HG[0]:IN_HG[1]])

    p = _dot(u, win_ref[:, IN_MLA[0]:IN_MLA[1]])
    cq = _rms(p[:, 0:MLA_Q_LORA], qg_ref[...]).astype(BF16)
    ckv = _rms(p[:, MLA_Q_LORA:MLA_Q_LORA + MLA_KV_LORA], kvg_ref[...]).astype(BF16)
    q = _dot(cq, wuq_ref[...])
    kv = _dot(ckv, wukv_ref[...])
    c, s1, s2 = c_ref[...], s1_ref[...], s2_ref[...]

    def rope(x):
        return x * c + pltpu.roll(x, LANES - MLA_ROPE // 2, 1) * s1 + pltpu.roll(x, MLA_ROPE // 2, 1) * s2

    kpe = rope(p[:, 384:512])
    for hh in range(N_HEADS):
        sl = slice(hh * LANES, (hh + 1) * LANES)
        qm_ref[:, sl] = rope(q[:, sl]).astype(BF16)
        km_ref[:, sl] = (kv[:, sl] + kpe).astype(BF16)
    vm_ref[...] = kv[:, N_HEADS * LANES:].astype(BF16)


def _in_proj(h, g1, win, qg, wuq, kvg, wukv, tabs):
    m = h.shape[0]
    tm = ROW_TILE
    rows = lambda w: pl.BlockSpec((tm, w), lambda i: (i, 0))
    full = lambda a: pl.BlockSpec(a.shape, lambda i: (0, 0))
    c, s1, s2 = tabs
    out_shape = (
        jax.ShapeDtypeStruct((m, N_HEADS * LANES), BF16),
        jax.ShapeDtypeStruct((m, N_HEADS * LANES), BF16),
        jax.ShapeDtypeStruct((m, GROUP_W), BF16),
        jax.ShapeDtypeStruct((m, 2 * LRU_WIDTH), F32),
        jax.ShapeDtypeStruct((m, 3 * GROUP_W), BF16),
        jax.ShapeDtypeStruct((m, 4 * GROUP_W), F32),
    )
    return pl.pallas_call(
        _in_proj_kernel,
        out_shape=out_shape,
        grid=(m // tm,),
        in_specs=[rows(D_MODEL), full(g1), full(win), full(qg), full(wuq), full(kvg), full(wukv),
                  rows(LANES), rows(LANES), rows(LANES)],
        out_specs=tuple(rows(s.shape[1]) for s in out_shape),
        compiler_params=_params("arbitrary"),
        name="in_proj",
    )(h, g1, win, qg, wuq, kvg, wukv, c, s1, s2)


def _mla_kernel(q_ref, k_ref, v_ref, gn_ref, o_ref, m_ref, l_ref, acc_ref):
    t = ATTN_TILE
    qi = pl.program_id(1)
    m_ref[...] = jnp.full(m_ref.shape, MASK_VALUE, F32)
    l_ref[...] = jnp.zeros(l_ref.shape, F32)
    acc_ref[...] = jnp.zeros(acc_ref.shape, F32)
    row = lax.broadcasted_iota(jnp.int32, (t, t), 0)
    col = lax.broadcasted_iota(jnp.int32, (t, t), 1)

    def block(j, diagonal):
        r0 = pl.multiple_of(j * t, t)
        for hh in range(N_HEADS):
            sl = slice(hh * LANES, (hh + 1) * LANES)
            vs = slice((hh // 2) * LANES, (hh // 2 + 1) * LANES)
            s = _dot_nt(q_ref[:, sl], k_ref[pl.ds(r0, t), sl])
            if diagonal:
                s = jnp.where(col <= row, s, MASK_VALUE)
            m_prev = m_ref[hh]
            m_new = jnp.maximum(m_prev, jnp.max(s, axis=1, keepdims=True))
            alpha = jnp.exp(m_prev - m_new)
            p = jnp.exp(s - m_new)
            l_ref[hh] = alpha * l_ref[hh] + jnp.sum(p, axis=1, keepdims=True)
            acc_ref[hh] = alpha * acc_ref[hh] + _dot(p.astype(BF16), v_ref[pl.ds(r0, t), vs])
            m_ref[hh] = m_new

    def body(j, carry):
        block(j, False)
        return carry

    lax.fori_loop(0, qi, body, 0)
    block(qi, True)

    lane = lax.broadcasted_iota(jnp.int32, (t, LANES), 1)
    pairs = []
    for pr in range(N_HEADS // 2):
        o0 = acc_ref[2 * pr] / l_ref[2 * pr]
        o1 = acc_ref[2 * pr + 1] / l_ref[2 * pr + 1]
        pairs.append(jnp.where(lane < MLA_V, o0, o1))
    y = jnp.concatenate(pairs, axis=1)
    o_ref[...] = _rms(y, gn_ref[...]).astype(BF16)


def _mla_attention(q, k, v, gn, batch):
    m = q.shape[0]
    s = m // batch
    t = ATTN_TILE
    nq = s // t
    return pl.pallas_call(
        _mla_kernel,
        out_shape=jax.ShapeDtypeStruct((m, GROUP_W), BF16),
        grid=(batch, nq),
        in_specs=[pl.BlockSpec((t, N_HEADS * LANES), lambda b, i: (b * nq + i, 0)),
                  pl.BlockSpec((s, N_HEADS * LANES), lambda b, i: (b, 0)),
                  pl.BlockSpec((s, GROUP_W), lambda b, i: (b, 0)),
                  pl.BlockSpec((1, GROUP_W), lambda b, i: (0, 0))],
        out_specs=pl.BlockSpec((t, GROUP_W), lambda b, i: (b * nq + i, 0)),
        scratch_shapes=[pltpu.VMEM((N_HEADS, t, 1), F32),
                        pltpu.VMEM((N_HEADS, t, 1), F32),
                        pltpu.VMEM((N_HEADS, t, LANES), F32)],
        compiler_params=_params("arbitrary", "arbitrary"),
        name="mla_attention",
    )(q, k, v, gn)


def _sb_kernel(q_ref, k_ref, v_ref, tri_ref, gn_ref, o_ref, qm_ref, carry_ref, acc_ref):
    t = ATTN_TILE
    qi = pl.program_id(1)
    lane = lax.broadcasted_iota(jnp.int32, (t, LANES), 1)
    scale = HEAD_DIM ** -0.5
    for hh in range(N_HEADS):
        qp = q_ref[:, (hh // 2) * LANES:(hh // 2 + 1) * LANES]
        own = (lane >= (hh % 2) * HEAD_DIM) & (lane < (hh % 2 + 1) * HEAD_DIM)
        qm_ref[hh] = jnp.where(own, qp * scale, 0).astype(BF16)
    carry_ref[...] = jnp.zeros(carry_ref.shape, F32)
    acc_ref[...] = jnp.zeros(acc_ref.shape, F32)
    row = lax.broadcasted_iota(jnp.int32, (t, t), 0)
    col = lax.broadcasted_iota(jnp.int32, (t, t), 1)
    strict = col < row

    def block(j, diagonal):
        r0 = pl.multiple_of(j * t, t)
        for hh in range(N_HEADS):
            ps = slice((hh // 2) * LANES, (hh // 2 + 1) * LANES)
            z = _dot_nt(qm_ref[hh], k_ref[pl.ds(r0, t), ps])
            log_beta = jnp.minimum(z, 0.0) - jnp.log(1.0 + jnp.exp(-jnp.abs(z)))
            log_keep = log_beta - z
            if diagonal:
                log_keep = jnp.where(strict, log_keep, 0.0)
            hi, lo = _split_bf16(log_keep)
            later = _dot(hi, tri_ref[...]) + _dot(lo, tri_ref[...])
            w = jnp.exp(log_beta + later + carry_ref[hh])
            if diagonal:
                w = jnp.where(strict, w, 0.0)
            acc_ref[hh] += _dot(w.astype(BF16), v_ref[pl.ds(r0, t), ps])
            carry_ref[hh] += jnp.sum(log_keep, axis=1, keepdims=True)

    block(qi, True)

    def body(step, carry):
        block(qi - 1 - step, False)
        return carry

    lax.fori_loop(0, qi, body, 0)

    pairs = []
    for pr in range(N_HEADS // 2):
        pairs.append(jnp.where(lane < HEAD_DIM, acc_ref[2 * pr], acc_ref[2 * pr + 1]))
    y = jnp.concatenate(pairs, axis=1)
    o_ref[...] = _rms(y, gn_ref[...]).astype(BF16)


def _sb_attention(qkv, gn, batch):
    m = qkv.shape[0]
    s = m // batch
    t = ATTN_TILE
    nq = s // t
    idx = np.arange(t)
    tri = jnp.asarray(idx[:, None] > idx[None, :], BF16)
    return pl.pallas_call(
        _sb_kernel,
        out_shape=jax.ShapeDtypeStruct((m, GROUP_W), BF16),
        grid=(batch, nq),
        in_specs=[pl.BlockSpec((t, GROUP_W), lambda b, i: (b * nq + i, 0)),
                  pl.BlockSpec((s, GROUP_W), lambda b, i: (b, 1)),
                  pl.BlockSpec((s, GROUP_W), lambda b, i: (b, 2)),
                  pl.BlockSpec((t, t), lambda b, i: (0, 0)),
                  pl.BlockSpec((1, GROUP_W), lambda b, i: (0, 0))],
        out_specs=pl.BlockSpec((t, GROUP_W), lambda b, i: (b * nq + i, 0)),
        scratch_shapes=[pltpu.VMEM((N_HEADS, t, LANES), BF16),
                        pltpu.VMEM((N_HEADS, t, 1), F32),
                        pltpu.VMEM((N_HEADS, t, LANES), F32)],
        compiler_params=_params("arbitrary", "arbitrary"),
        name="sb_attention",
    )(qkv, qkv, qkv, tri, gn)


def _lru_kernel(x_ref, gate_ref, cw_ref, cb_ref, wab_ref, ba_ref, bx_ref, lam_ref, gn_ref,
                o_ref, xs_ref, a_ref, u_ref, h_ref):
    ts, nb, w = x_ref.shape
    pad = CONV_WIDTH - 1

    @pl.when(pl.program_id(0) == 0)
    def _():
        xs_ref[0:pad] = jnp.zeros((pad, nb, w), F32)
        h_ref[...] = jnp.zeros(h_ref.shape, F32)

    @pl.when(pl.program_id(0) > 0)
    def _():
        xs_ref[0:pad] = xs_ref[ts:ts + pad]

    xs_ref[pad:pad + ts] = x_ref[...]
    xc = cb_ref[...].reshape(1, 1, w)
    for i in range(CONV_WIDTH):
        xc = xc + cw_ref[i:i + 1, :].reshape(1, 1, w) * xs_ref[i:i + ts]
    flat = xc.reshape(ts * nb, w)
    gates = _dot(flat.astype(BF16), wab_ref[...])
    rec_gate = _sigmoid(gates[:, :w] + ba_ref[...])
    in_gate = _sigmoid(gates[:, w:] + bx_ref[...])
    lam = lam_ref[...]
    softplus_neg_lam = jnp.maximum(-lam, 0.0) + jnp.log1p(jnp.exp(-jnp.abs(lam)))
    log_a = -LRU_C * rec_gate * softplus_neg_lam
    a = jnp.exp(log_a)
    a_ref[...] = a.reshape(ts, nb, w)
    one_minus_a2 = -jnp.tanh(log_a) * (a * a + 1.0)
    u_ref[...] = (jnp.sqrt(jnp.maximum(one_minus_a2, 0.0)) * (in_gate * flat)).reshape(ts, nb, w)

    def step(i, h):
        h = a_ref[i] * h + u_ref[i]
        u_ref[i] = h
        return h

    h_ref[...] = lax.fori_loop(0, ts, step, h_ref[...], unroll=8)

    g = gate_ref[...]
    gelu = 0.5 * g * (1.0 + jnp.tanh(math.sqrt(2.0 / math.pi) * (g + 0.044715 * (g * g * g))))
    o_ref[...] = _rms(u_ref[...] * gelu, gn_ref[...].reshape(1, 1, w))


def _rglru(x_t, gate_t, cw, cb, wab, ba, bx, lam, gn):
    s, nb, w = x_t.shape
    ts = LRU_TIME_TILE
    tile = pl.BlockSpec((ts, nb, w), lambda i: (i, 0, 0))
    full = lambda a: pl.BlockSpec(a.shape, lambda i: (0,) * a.ndim)
    return pl.pallas_call(
        _lru_kernel,
        out_shape=jax.ShapeDtypeStruct((s, nb, w), F32),
        grid=(s // ts,),
        in_specs=[tile, tile, full(cw), full(cb), full(wab), full(ba), full(bx), full(lam), full(gn)],
        out_specs=tile,
        scratch_shapes=[pltpu.VMEM((ts + CONV_WIDTH - 1, nb, w), F32),
                        pltpu.VMEM((ts, nb, w), F32),
                        pltpu.VMEM((ts, nb, w), F32),
                        pltpu.VMEM((nb, w), F32)],
        compiler_params=_params("arbitrary"),
        name="rglru",
    )(x_t, gate_t, cw, cb, wab, ba, bx, lam, gn)


def _hg_constants():
    t = HG_CHUNK
    idx = np.arange(t)
    sums = [idx[:, None] >= idx[None, :]]
    masks = [idx[:, None] == idx[None, :]]
    for lev in range(HG_LEVELS):
        half = 1 << lev
        mid = (idx // (2 * half)) * (2 * half) + half
        upper = idx >= mid
        j = idx[None, :]
        up_rows = upper[:, None] & (j >= mid[:, None]) & (j <= idx[:, None])
        low_rows = (~upper)[:, None] & (j > idx[:, None]) & (j < mid[:, None])
        sums.append(up_rows | low_rows)
        same = (idx[:, None] // (2 * half)) == (idx[None, :] // (2 * half))
        masks.append(same & upper[:, None] & (~upper)[None, :])
    sum_mat = jnp.asarray(np.concatenate(sums, axis=0), BF16)
    mask = jnp.asarray(np.stack(masks, axis=0), F32)
    head = np.arange(GROUP_W) // HEAD_DIM
    same_head = jnp.asarray(head[:, None] == head[None, :], BF16)
    return sum_mat, mask, same_head


def _hg_kernel(x_ref, lbl_ref, ng_ref, sum_ref, mask_ref, same_ref, o_ref, st_ref, *, layer):
    t = HG_CHUNK
    w = GROUP_W

    @pl.when(pl.program_id(1) == 0)
    def _():
        st_ref[...] = jnp.zeros(st_ref.shape, F32)

    q = x_ref[:, 0:w]
    fz = x_ref[:, w:2 * w]
    v = x_ref[:, 2 * w:3 * w]
    g = x_ref[:, 3 * w:4 * w]

    logits = lbl_ref[...]
    ex = jnp.exp(logits - jnp.max(logits, axis=0, keepdims=True))
    prob = ex / jnp.sum(ex, axis=0, keepdims=True)
    csum = prob[0:1, :]
    for i in range(1, layer + 1):
        csum = csum + prob[i:i + 1, :]
    lb = csum - prob[0:1, :]

    a_ = jnp.log(jnp.maximum(lb, LB_FLOOR))
    b_ = jnp.log1p(-lb) + _log_sigmoid(fz)
    log_f = jnp.maximum(a_, b_) + jnp.log1p(jnp.exp(-jnp.abs(a_ - b_)))
    key = (1.0 - lb) * _sigmoid(-fz)

    hi, lo = _split_bf16(log_f)
    sums = _dot(sum_ref[...], hi) + _dot(sum_ref[...], lo)
    cum = sums[0:t]
    last = cum[t - 1:t, :]

    st = st_ref[...]
    out = _dot_nt((q * jnp.exp(cum)).astype(BF16), st.astype(BF16))
    vb = v.astype(BF16)
    kdec = (key * jnp.exp(last - cum)).astype(BF16)
    same = same_ref[...]
    st_ref[...] = st * jnp.exp(last) + jnp.where(same > 0, _dot_tn(vb, kdec), 0.0)

    lane_head = lax.broadcasted_iota(jnp.int32, (t, w), 1) // HEAD_DIM
    scores = [jnp.zeros((t, t), F32) for _ in range(N_HEADS)]
    for lev in range(HG_LEVELS + 1):
        if lev == 0:
            qt, kt = q, key
        else:
            decay = jnp.exp(sums[lev * t:(lev + 1) * t])
            qt, kt = q * decay, key * decay
        kt = kt.astype(BF16)
        for hh in range(N_HEADS):
            qh = jnp.where(lane_head == hh, qt, 0.0).astype(BF16)
            scores[hh] = scores[hh] + _dot_nt(qh, kt) * mask_ref[lev]
    for hh in range(N_HEADS):
        out = out + jnp.where(lane_head == hh, _dot(scores[hh].astype(BF16), vb), 0.0)

    sq_hi, sq_lo = _split_bf16(out * out)
    mean_sq = (_dot(sq_hi, same) + _dot(sq_lo, same)) * (1.0 / HEAD_DIM)
    normed = out * lax.rsqrt(mean_sq + NORM_EPS) * ng_ref[...]
    o_ref[...] = (normed * (g * _sigmoid(g))).astype(BF16)


def _hgrn2(x, lb_logits, ng, layer, batch):
    m = x.shape[0]
    t = HG_CHUNK
    nc = m // batch // t
    sum_mat, mask, same_head = _hg_constants()
    full = lambda a: pl.BlockSpec(a.shape, lambda b, c: (0,) * a.ndim)
    return pl.pallas_call(
        functools.partial(_hg_kernel, layer=layer),
        out_shape=jax.ShapeDtypeStruct((m, GROUP_W), BF16),
        grid=(batch, nc),
        in_specs=[pl.BlockSpec((t, 4 * GROUP_W), lambda b, c: (b * nc + c, 0)),
                  full(lb_logits), full(ng), full(sum_mat), full(mask), full(same_head)],
        out_specs=pl.BlockSpec((t, GROUP_W), lambda b, c: (b * nc + c, 0)),
        scratch_shapes=[pltpu.VMEM((GROUP_W, GROUP_W), F32)],
        compiler_params=_params("arbitrary", "arbitrary"),
        name="hgrn2",
    )(x, lb_logits, ng, sum_mat, mask, same_head)


def _out_ffn_kernel(h_ref, ya_ref, yb_ref, yc_ref, yd_ref, wout_ref, g2_ref, wg_ref, wu_ref, wd_ref,
                    gf_ref, o_ref, *, final):
    mixed = jnp.concatenate([ya_ref[...], yb_ref[...], yc_ref[...], yd_ref[...]], axis=1)
    h1 = h_ref[...] + _dot(mixed, wout_ref[...])
    un = _rms(h1, g2_ref[...]).astype(BF16)
    acc = h1
    for c in range(FFN_HIDDEN // FFN_CHUNK):
        sl = slice(c * FFN_CHUNK, (c + 1) * FFN_CHUNK)
        gt = _dot(un, wg_ref[:, sl])
        up = _dot(un, wu_ref[:, sl])
        act = (gt * _sigmoid(gt) * up).astype(BF16)
        acc = acc + _dot(act, wd_ref[sl, :])
    if final:
        acc = _rms(acc, gf_ref[...])
    o_ref[...] = acc


def _out_ffn(h, ya, yb, yc, yd, wout, g2, wg, wu, wd, gf, final):
    m = h.shape[0]
    tm = ROW_TILE
    rows = lambda w: pl.BlockSpec((tm, w), lambda i: (i, 0))
    once = lambda a: pl.BlockSpec(a.shape, lambda i: (0, 0), pipeline_mode=pl.Buffered(1))
    return pl.pallas_call(
        functools.partial(_out_ffn_kernel, final=final),
        out_shape=jax.ShapeDtypeStruct((m, D_MODEL), F32),
        grid=(m // tm,),
        in_specs=[rows(D_MODEL), rows(GROUP_W), rows(GROUP_W), rows(GROUP_W), rows(GROUP_W),
                  once(wout), once(g2), once(wg), once(wu), once(wd), once(gf)],
        out_specs=rows(D_MODEL),
        compiler_params=_params("arbitrary"),
        name="out_ffn",
    )(h, ya, yb, yc, yd, wout, g2, wg, wu, wd, gf)


def _pad_in_proj(w):
    d = w.shape[0]
    z = lambda n: jnp.zeros((d, n), w.dtype)
    return jnp.concatenate([w[:, :384], z(MLA_NOPE), w[:, 384:416], z(LANES - MLA_NOPE - MLA_ROPE), w[:, 416:]],
                           axis=1).astype(BF16)


def _pad_heads(w, width):
    k = w.shape[0]
    w = w.reshape(k, N_HEADS, width)
    w = jnp.pad(w, ((0, 0), (0, 0), (0, LANES - width)))
    return w.reshape(k, N_HEADS * LANES)


def _block_diag(w):
    n, c, d = w.shape
    eye = jnp.eye(n, dtype=w.dtype)
    return (eye[:, None, :, None] * w[:, :, None, :]).reshape(n * c, n * d)


def kernel(x, positions, ln1_g, w_in, mla_q_norm_g, mla_w_uq, mla_kv_norm_g, mla_w_ukv, lru_conv_w, lru_conv_b, lru_w_a, lru_b_a, lru_w_x, lru_b_x, lru_lambda, hgrn_lb_logits, hgrn_norm_g, group_norm_g, w_out, ln2_g, w_ffn_gate, w_ffn_up, w_ffn_down, final_norm_g):
    batch, seq, d_model = x.shape
    depth = w_in.shape[0]
    m = batch * seq
    row = lambda v: v.reshape(1, -1)
    h = x.reshape(m, d_model)
    tabs = _rope_tables(positions)
    qk_scale = (MLA_NOPE + MLA_ROPE) ** -0.5
    for l in range(depth):
        wuq = _pad_heads(mla_w_uq[l] * qk_scale, MLA_NOPE + MLA_ROPE).astype(BF16)
        wukv = mla_w_ukv[l].reshape(MLA_KV_LORA, N_HEADS, MLA_NOPE + MLA_V)
        wuk = _pad_heads(wukv[:, :, :MLA_NOPE].reshape(MLA_KV_LORA, -1), MLA_NOPE)
        wuv = wukv[:, :, MLA_NOPE:].reshape(MLA_KV_LORA, -1)
        wukv_p = jnp.concatenate([wuk, wuv], axis=1).astype(BF16)
        qm, km, vm, lru_in, sb_in, hg_in = _in_proj(
            h, row(ln1_g[l]), _pad_in_proj(w_in[l]), row(mla_q_norm_g[l]), wuq,
            row(mla_kv_norm_g[l]), wukv_p, tabs)

        y_a = _mla_attention(qm, km, vm, row(group_norm_g[l, 0]), batch)

        to_time_major = lambda a: a.reshape(batch, seq, LRU_WIDTH).transpose(1, 0, 2)
        wab = jnp.concatenate([_block_diag(lru_w_a[l]), _block_diag(lru_w_x[l])], axis=1).astype(BF16)
        y_b = _rglru(to_time_major(lru_in[:, :LRU_WIDTH]), to_time_major(lru_in[:, LRU_WIDTH:]),
                     lru_conv_w[l], row(lru_conv_b[l]), wab, row(lru_b_a[l]), row(lru_b_x[l]),
                     row(lru_lambda[l]), row(group_norm_g[l, 1]))
        y_b = y_b.transpose(1, 0, 2).reshape(m, LRU_WIDTH).astype(BF16)

        y_c = _sb_attention(sb_in, row(group_norm_g[l, 2]), batch)
        y_d = _hgrn2(hg_in, hgrn_lb_logits, row(hgrn_norm_g[l]), l, batch)

        h = _out_ffn(h, y_a, y_b, y_c, y_d, w_out[l].astype(BF16), row(ln2_g[l]),
                     w_ffn_gate[l].astype(BF16), w_ffn_up[l].astype(BF16), w_ffn_down[l].astype(BF16),
                     row(final_norm_g), final=(l == depth - 1))
    return h.reshape(batch, seq, d_model)
```

```python
import functools
import math

import numpy as np
import jax
import jax.numpy as jnp
from jax import lax
from jax.experimental import pallas as pl
from jax.experimental.pallas import tpu as pltpu

F32 = jnp.float32
BF16 = jnp.bfloat16

D_MODEL = 1024
N_HEADS = 4
MLA_Q_LORA = 256
MLA_KV_LORA = 128
MLA_NOPE = 64
MLA_ROPE = 32
MLA_V = 64
ROPE_THETA = 10000.0
LRU_WIDTH = 256
CONV_WIDTH = 4
LRU_C = 8.0
HEAD_DIM = 64
GROUP_W = 256
FFN_HIDDEN = 2816
NORM_EPS = 1e-6
MASK_VALUE = -1e30
LB_FLOOR = 1e-30

LANES = 128
VMEM_LIMIT_BYTES = 56 * 1024 * 1024

ROW_TILE = 512
ATTN_TILE = 256
SOFTMAX_ROWS = 64
SUM_ROWS = 16
LOG2_E = math.log2(math.e)
LRU_TIME_TILE = 256
HG_CHUNK = 128
HG_LEVELS = int(math.log2(HG_CHUNK))
FFN_CHUNK = 1408

IN_MLA = (0, 512)
IN_LRU = (512, 1024)
IN_SB = (1024, 1792)
IN_HG = (1792, 2816)
D_IN_PAD = 2816


def _params(*semantics):
    return pltpu.CompilerParams(dimension_semantics=semantics, vmem_limit_bytes=VMEM_LIMIT_BYTES)


def _rms(x, g):
    return x * lax.rsqrt(jnp.mean(x * x, axis=-1, keepdims=True) + NORM_EPS) * g


def _dot(a, b):
    return jnp.dot(a, b, preferred_element_type=F32)


def _dot_nt(a, b):
    return lax.dot_general(a, b, (((1,), (1,)), ((), ())), preferred_element_type=F32)


def _dot_tn(a, b):
    return lax.dot_general(a, b, (((0,), (0,)), ((), ())), preferred_element_type=F32)


def _split_bf16(x):
    hi = x.astype(BF16)
    lo = (x - hi.astype(F32)).astype(BF16)
    return hi, lo


def _sigmoid(x):
    return 1.0 / (1.0 + jnp.exp(-x))


def _log_sigmoid(x):
    return jnp.minimum(x, 0.0) - jnp.log1p(jnp.exp(-jnp.abs(x)))


def _rope_table_kernel(pos_ref, freq_ref, m1_ref, m2_ref, c_ref, s1_ref, s2_ref):
    ang = pos_ref[...] * freq_ref[...]
    sn = jnp.sin(ang)
    c_ref[...] = jnp.cos(ang)
    s1_ref[...] = sn * m1_ref[...]
    s2_ref[...] = sn * m2_ref[...]


def _rope_tables(positions):
    m = positions.size
    half = MLA_ROPE // 2
    inv_freq = ROPE_THETA ** (-jnp.arange(half, dtype=F32) / half)
    zeros = jnp.zeros((MLA_NOPE,), F32)
    tail = jnp.zeros((LANES - MLA_NOPE - MLA_ROPE,), F32)
    freq = jnp.concatenate([zeros, inv_freq, inv_freq, tail]).reshape(1, LANES)
    ones = jnp.ones((half,), F32)
    m1 = jnp.concatenate([zeros, -ones, 0 * ones, tail]).reshape(1, LANES)
    m2 = jnp.concatenate([zeros, 0 * ones, ones, tail]).reshape(1, LANES)
    pos = positions.astype(F32).reshape(m, 1)
    tm = ROW_TILE
    row = pl.BlockSpec((1, LANES), lambda i: (0, 0))
    tab = pl.BlockSpec((tm, LANES), lambda i: (i, 0))
    shape = jax.ShapeDtypeStruct((m, LANES), F32)
    return pl.pallas_call(
        _rope_table_kernel,
        out_shape=(shape, shape, shape),
        grid=(m // tm,),
        in_specs=[pl.BlockSpec((tm, 1), lambda i: (i, 0)), row, row, row],
        out_specs=(tab, tab, tab),
        compiler_params=_params("arbitrary"),
        name="rope_tables",
    )(pos, freq, m1, m2)


def _in_proj_kernel(h_ref, g1_ref, win_ref, qg_ref, wuq_ref, kvg_ref, wukv_ref,
                    c_ref, s1_ref, s2_ref,
                    qm_ref, km_ref, vmt_ref, lru_ref, sb_ref, sbvt_ref, hg_ref):
    t = ATTN_TILE
    n_blocks = h_ref.shape[0] // t

    def store_transposed(ref, v):
        for blk in range(n_blocks):
            ref[blk] = v[blk * t:(blk + 1) * t, :].T.astype(BF16)

    u = _rms(h_ref[...], g1_ref[...]).astype(BF16)
    lru_ref[...] = _dot(u, win_ref[:, IN_LRU[0]:IN_LRU[1]])
    sb = _dot(u, win_ref[:, IN_SB[0]:IN_SB[1]])
    sb_ref[...] = sb[:, :2 * GROUP_W].astype(BF16)
    store_transposed(sbvt_ref, sb[:, 2 * GROUP_W:])
    hg_ref[...] = _dot(u, win_ref[:, IN_HG[0]:IN_HG[1]])

    p = _dot(u, win_ref[:, IN_MLA[0]:IN_MLA[1]])
    cq = _rms(p[:, 0:MLA_Q_LORA], qg_ref[...]).astype(BF16)
    ckv = _rms(p[:, MLA_Q_LORA:MLA_Q_LORA + MLA_KV_LORA], kvg_ref[...]).astype(BF16)
    q = _dot(cq, wuq_ref[...])
    kv = _dot(ckv, wukv_ref[...])
    c, s1, s2 = c_ref[...], s1_ref[...], s2_ref[...]

    def rope(x):
        return x * c + pltpu.roll(x, LANES - MLA_ROPE // 2, 1) * s1 + pltpu.roll(x, MLA_ROPE // 2, 1) * s2

    kpe = rope(p[:, 384:512])
    for hh in range(N_HEADS):
        sl = slice(hh * LANES, (hh + 1) * LANES)
        qm_ref[:, sl] = rope(q[:, sl]).astype(BF16)
        km_ref[:, sl] = (kv[:, sl] + kpe).astype(BF16)
    store_transposed(vmt_ref, kv[:, N_HEADS * LANES:])


def _in_proj(h, g1, win, qg, wuq, kvg, wukv, tabs):
    m = h.shape[0]
    tm = ROW_TILE
    t = ATTN_TILE
    rows = lambda w: pl.BlockSpec((tm, w), lambda i: (i, 0))
    full = lambda a: pl.BlockSpec(a.shape, lambda i: (0, 0))
    c, s1, s2 = tabs
    v_t = jax.ShapeDtypeStruct((m // t, GROUP_W, t), BF16)
    out_shape = (
        jax.ShapeDtypeStruct((m, N_HEADS * LANES), BF16),
        jax.ShapeDtypeStruct((m, N_HEADS * LANES), BF16),
        v_t,
        jax.ShapeDtypeStruct((m, 2 * LRU_WIDTH), F32),
        jax.ShapeDtypeStruct((m, 2 * GROUP_W), BF16),
        v_t,
        jax.ShapeDtypeStruct((m, 4 * GROUP_W), F32),
    )

    def out_spec(sd):
        if len(sd.shape) == 3:
            return pl.BlockSpec((tm // t, GROUP_W, t), lambda i: (i, 0, 0))
        return rows(sd.shape[1])
    return pl.pallas_call(
        _in_proj_kernel,
        out_shape=out_shape,
        grid=(m // tm,),
        in_specs=[rows(D_MODEL), full(g1), full(win), full(qg), full(wuq), full(kvg), full(wukv),
                  rows(LANES), rows(LANES), rows(LANES)],
        out_specs=tuple(out_spec(sd) for sd in out_shape),
        compiler_params=_params("arbitrary"),
        name="in_proj",
    )(h, g1, win, qg, wuq, kvg, wukv, c, s1, s2)


def _mla_kernel(q_ref, k_ref, vt_ref, gn_ref, o_ref, m_ref, acc_ref, s0_ref, s1_ref, p_ref):
    t = ATTN_TILE
    qi = pl.program_id(1)
    m_ref[...] = jnp.full(m_ref.shape, MASK_VALUE, F32)
    acc_ref[...] = jnp.zeros(acc_ref.shape, F32)
    n_chunks = t // SOFTMAX_ROWS
    key_idx = lax.broadcasted_iota(jnp.int32, (SOFTMAX_ROWS, t), 0)
    qry_idx = lax.broadcasted_iota(jnp.int32, (SOFTMAX_ROWS, t), 1)
    ones_rows = jnp.ones((SUM_ROWS, t), BF16)

    def scores(j, hh, s_ref):
        r0 = pl.multiple_of(j * t, t)
        sl = slice(hh * LANES, (hh + 1) * LANES)
        s_ref[hh] = _dot_nt(k_ref[pl.ds(r0, t), sl], q_ref[:, sl])

    def softmax_pv(j, hh, s_ref, diagonal):
        def chunk(c):
            s = s_ref[hh, c * SOFTMAX_ROWS:(c + 1) * SOFTMAX_ROWS, :]
            if diagonal:
                s = jnp.where(key_idx + c * SOFTMAX_ROWS <= qry_idx, s, MASK_VALUE)
            return s

        m_prev = m_ref[hh]
        m_blk = chunk(0)
        for c in range(1, n_chunks):
            m_blk = jnp.maximum(m_blk, chunk(c))
        m_new = jnp.maximum(m_prev, jnp.max(m_blk, axis=0, keepdims=True))
        alpha = jnp.exp2(m_prev - m_new)
        for c in range(n_chunks):
            p_ref[hh, c * SOFTMAX_ROWS:(c + 1) * SOFTMAX_ROWS, :] = jnp.exp2(chunk(c) - m_new).astype(BF16)
        v_ext = jnp.concatenate([vt_ref[j, hh * MLA_V:(hh + 1) * MLA_V, :], ones_rows], axis=0)
        acc_ref[hh] = alpha * acc_ref[hh] + _dot(v_ext, p_ref[hh])
        m_ref[hh] = m_new

    def stage(j_next, next_ref, j_cur, cur_ref, diagonal=False):
        for hh in range(N_HEADS):
            if next_ref is not None:
                scores(j_next, hh, next_ref)
            if cur_ref is not None:
                softmax_pv(j_cur, hh, cur_ref, diagonal)

    stage(0, s0_ref, None, None)

    def body(pair, carry):
        j = 2 * pair
        stage(j + 1, s1_ref, j, s0_ref)
        stage(j + 2, s0_ref, j + 1, s1_ref)
        return carry

    lax.fori_loop(0, qi // 2, body, 0)

    @pl.when(qi % 2 == 0)
    def _():
        stage(None, None, qi, s0_ref, True)

    @pl.when(qi % 2 == 1)
    def _():
        stage(qi, s1_ref, qi - 1, s0_ref)
        stage(None, None, qi, s1_ref, True)

    y_t = jnp.concatenate([acc_ref[hh, 0:MLA_V, :] / acc_ref[hh, MLA_V:MLA_V + 1, :] for hh in range(N_HEADS)],
                          axis=0)
    o_ref[...] = _rms(y_t.T, gn_ref[...]).astype(BF16)


def _mla_attention(q, k, v_t, gn, batch):
    m = q.shape[0]
    s = m // batch
    t = ATTN_TILE
    nq = s // t
    return pl.pallas_call(
        _mla_kernel,
        out_shape=jax.ShapeDtypeStruct((m, GROUP_W), BF16),
        grid=(batch, nq),
        in_specs=[pl.BlockSpec((t, N_HEADS * LANES), lambda b, i: (b * nq + i, 0)),
                  pl.BlockSpec((s, N_HEADS * LANES), lambda b, i: (b, 0)),
                  pl.BlockSpec((nq, GROUP_W, t), lambda b, i: (b, 0, 0)),
                  pl.BlockSpec((1, GROUP_W), lambda b, i: (0, 0))],
        out_specs=pl.BlockSpec((t, GROUP_W), lambda b, i: (b * nq + i, 0)),
        scratch_shapes=[pltpu.VMEM((N_HEADS, 1, t), F32),
                        pltpu.VMEM((N_HEADS, MLA_V + SUM_ROWS, t), F32),
                        pltpu.VMEM((N_HEADS, t, t), F32),
                        pltpu.VMEM((N_HEADS, t, t), F32),
                        pltpu.VMEM((N_HEADS, t, t), BF16)],
        compiler_params=_params("arbitrary", "arbitrary"),
        name="mla_attention",
    )(q, k, v_t, gn)


def _sb_kernel(q_ref, k_ref, vt_ref, tri_ref, gn_ref, o_ref,
               qm_ref, carry_ref, acc_ref, z0_ref, z1_ref, lb_ref, hl_ref, later_ref, w_ref):
    t = ATTN_TILE
    qi = pl.program_id(1)
    n_chunks = t // SOFTMAX_ROWS
    lane = lax.broadcasted_iota(jnp.int32, (t, LANES), 1)
    for hh in range(N_HEADS):
        qp = q_ref[:, (hh // 2) * LANES:(hh // 2 + 1) * LANES]
        own = (lane >= (hh % 2) * HEAD_DIM) & (lane < (hh % 2 + 1) * HEAD_DIM)
        qm_ref[hh] = jnp.where(own, qp, jnp.zeros_like(qp))
    carry_ref[...] = jnp.zeros(carry_ref.shape, F32)
    acc_ref[...] = jnp.zeros(acc_ref.shape, F32)
    key_idx = lax.broadcasted_iota(jnp.int32, (SOFTMAX_ROWS, t), 0)
    qry_idx = lax.broadcasted_iota(jnp.int32, (SOFTMAX_ROWS, t), 1)

    def rows(c):
        return slice(c * SOFTMAX_ROWS, (c + 1) * SOFTMAX_ROWS)

    def logits(step, hh, z_ref):
        r0 = pl.multiple_of(jnp.maximum(qi - step, 0) * t, t)
        ps = slice((hh // 2) * LANES, (hh // 2 + 1) * LANES)
        z_ref[hh] = _dot_nt(k_ref[pl.ds(r0, t), ps], qm_ref[hh])

    def log_terms(hh, z_ref, diagonal):
        for c in range(n_chunks):
            z = z_ref[hh, rows(c), :]
            softplus = jnp.log2(1.0 + jnp.exp2(-jnp.abs(z)))
            log_beta = jnp.minimum(z, 0.0) - softplus
            log_keep = log_beta - z
            if diagonal:
                log_keep = jnp.where(key_idx + c * SOFTMAX_ROWS < qry_idx, log_keep, 0.0)
            lb_ref[hh, rows(c), :] = log_beta
            hi, lo = _split_bf16(log_keep)
            hl_ref[hh, rows(c), :] = hi
            hl_ref[hh, slice(t + c * SOFTMAX_ROWS, t + (c + 1) * SOFTMAX_ROWS), :] = lo
        later_ref[hh] = _dot(tri_ref[...], hl_ref[hh])

    def weights_pv(step, hh, diagonal):
        carry = carry_ref[hh]
        for c in range(n_chunks):
            w = jnp.exp2(lb_ref[hh, rows(c), :] + later_ref[hh, rows(c), :] + carry)
            if diagonal:
                w = jnp.where(key_idx + c * SOFTMAX_ROWS < qry_idx, w, 0.0)
            w_ref[hh, rows(c), :] = w.astype(BF16)
        carry_ref[hh] = carry + later_ref[hh, t:t + 1, :]
        v_t = vt_ref[qi - step, hh * HEAD_DIM:(hh + 1) * HEAD_DIM, :]
        acc_ref[hh] += _dot(v_t, w_ref[hh])

    def stage(step, cur_ref, next_ref, diagonal=False):
        if next_ref is not None:
            for hh in range(N_HEADS):
                logits(step + 1, hh, next_ref)
        for hh in range(N_HEADS):
            log_terms(hh, cur_ref, diagonal)
        for hh in range(N_HEADS):
            weights_pv(step, hh, diagonal)

    for hh in range(N_HEADS):
        logits(0, hh, z0_ref)
    stage(0, z0_ref, z1_ref, True)

    def body(pair, carry):
        step = 1 + 2 * pair
        stage(step, z1_ref, z0_ref)
        stage(step + 1, z0_ref, z1_ref)
        return carry

    lax.fori_loop(0, qi // 2, body, 0)

    @pl.when(qi % 2 == 1)
    def _():
        stage(qi, z1_ref, None)

    y_t = jnp.concatenate([acc_ref[hh] for hh in range(N_HEADS)], axis=0)
    o_ref[...] = _rms(y_t.T, gn_ref[...]).astype(BF16)


def _sb_attention(qk, v_t, gn, batch):
    m = qk.shape[0]
    s = m // batch
    t = ATTN_TILE
    nq = s // t
    idx = np.arange(t)
    later = idx[None, :] > idx[:, None]
    total = np.ones((SUM_ROWS, t), bool)
    tri = np.concatenate([later, total], axis=0)
    tri = jnp.asarray(np.concatenate([tri, tri], axis=1), BF16)
    return pl.pallas_call(
        _sb_kernel,
        out_shape=jax.ShapeDtypeStruct((m, GROUP_W), BF16),
        grid=(batch, nq),
        in_specs=[pl.BlockSpec((t, GROUP_W), lambda b, i: (b * nq + i, 0)),
                  pl.BlockSpec((s, GROUP_W), lambda b, i: (b, 1)),
                  pl.BlockSpec((nq, GROUP_W, t), lambda b, i: (b, 0, 0)),
                  pl.BlockSpec(tri.shape, lambda b, i: (0, 0)),
                  pl.BlockSpec((1, GROUP_W), lambda b, i: (0, 0))],
        out_specs=pl.BlockSpec((t, GROUP_W), lambda b, i: (b * nq + i, 0)),
        scratch_shapes=[pltpu.VMEM((N_HEADS, t, LANES), BF16),
                        pltpu.VMEM((N_HEADS, 1, t), F32),
                        pltpu.VMEM((N_HEADS, HEAD_DIM, t), F32),
                        pltpu.VMEM((N_HEADS, t, t), F32),
                        pltpu.VMEM((N_HEADS, t, t), F32),
                        pltpu.VMEM((N_HEADS, t, t), F32),
                        pltpu.VMEM((N_HEADS, 2 * t, t), BF16),
                        pltpu.VMEM((N_HEADS, t + SUM_ROWS, t), F32),
                        pltpu.VMEM((N_HEADS, t, t), BF16)],
        compiler_params=_params("arbitrary", "arbitrary"),
        name="sb_attention",
    )(qk, qk, v_t, tri, gn)


def _lru_kernel(x_ref, gate_ref, cw_ref, cb_ref, wab_ref, ba_ref, bx_ref, lam_ref, gn_ref,
                o_ref, xs_ref, a_ref, u_ref, h_ref):
    ts, nb, w = x_ref.shape
    pad = CONV_WIDTH - 1

    @pl.when(pl.program_id(0) == 0)
    def _():
        xs_ref[0:pad] = jnp.zeros((pad, nb, w), F32)
        h_ref[...] = jnp.zeros(h_ref.shape, F32)

    @pl.when(pl.program_id(0) > 0)
    def _():
        xs_ref[0:pad] = xs_ref[ts:ts + pad]

    xs_ref[pad:pad + ts] = x_ref[...]
    xc = cb_ref[...].reshape(1, 1, w)
    for i in range(CONV_WIDTH):
        xc = xc + cw_ref[i:i + 1, :].reshape(1, 1, w) * xs_ref[i:i + ts]
    flat = xc.reshape(ts * nb, w)
    gates = _dot(flat.astype(BF16), wab_ref[...])
    rec_gate = _sigmoid(gates[:, :w] + ba_ref[...])
    in_gate = _sigmoid(gates[:, w:] + bx_ref[...])
    lam = lam_ref[...]
    softplus_neg_lam = jnp.maximum(-lam, 0.0) + jnp.log1p(jnp.exp(-jnp.abs(lam)))
    log_a = -LRU_C * rec_gate * softplus_neg_lam
    a = jnp.exp(log_a)
    a_ref[...] = a.reshape(ts, nb, w)
    one_minus_a2 = -jnp.tanh(log_a) * (a * a + 1.0)
    u_ref[...] = (jnp.sqrt(jnp.maximum(one_minus_a2, 0.0)) * (in_gate * flat)).reshape(ts, nb, w)

    def step(i, h):
        h = a_ref[i] * h + u_ref[i]
        u_ref[i] = h
        return h

    h_ref[...] = lax.fori_loop(0, ts, step, h_ref[...], unroll=8)

    g = gate_ref[...]
    gelu = 0.5 * g * (1.0 + jnp.tanh(math.sqrt(2.0 / math.pi) * (g + 0.044715 * (g * g * g))))
    o_ref[...] = _rms(u_ref[...] * gelu, gn_ref[...].reshape(1, 1, w))


def _rglru(x_t, gate_t, cw, cb, wab, ba, bx, lam, gn):
    s, nb, w = x_t.shape
    ts = LRU_TIME_TILE
    tile = pl.BlockSpec((ts, nb, w), lambda i: (i, 0, 0))
    full = lambda a: pl.BlockSpec(a.shape, lambda i: (0,) * a.ndim)
    return pl.pallas_call(
        _lru_kernel,
        out_shape=jax.ShapeDtypeStruct((s, nb, w), F32),
        grid=(s // ts,),
        in_specs=[tile, tile, full(cw), full(cb), full(wab), full(ba), full(bx), full(lam), full(gn)],
        out_specs=tile,
        scratch_shapes=[pltpu.VMEM((ts + CONV_WIDTH - 1, nb, w), F32),
                        pltpu.VMEM((ts, nb, w), F32),
                        pltpu.VMEM((ts, nb, w), F32),
                        pltpu.VMEM((nb, w), F32)],
        compiler_params=_params("arbitrary"),
        name="rglru",
    )(x_t, gate_t, cw, cb, wab, ba, bx, lam, gn)


def _hg_constants():
    t = HG_CHUNK
    idx = np.arange(t)
    sums = [idx[:, None] >= idx[None, :]]
    masks = [idx[:, None] == idx[None, :]]
    for lev in range(HG_LEVELS):
        half = 1 << lev
        mid = (idx // (2 * half)) * (2 * half) + half
        upper = idx >= mid
        j = idx[None, :]
        up_rows = upper[:, None] & (j >= mid[:, None]) & (j <= idx[:, None])
        low_rows = (~upper)[:, None] & (j > idx[:, None]) & (j < mid[:, None])
        sums.append(up_rows | low_rows)
        same = (idx[:, None] // (2 * half)) == (idx[None, :] // (2 * half))
        masks.append(same & upper[:, None] & (~upper)[None, :])
    sum_mat = jnp.asarray(np.concatenate(sums, axis=0), BF16)
    mask = jnp.asarray(np.stack(masks, axis=0), F32)
    head = np.arange(GROUP_W) // HEAD_DIM
    same_head = jnp.asarray(head[:, None] == head[None, :], BF16)
    return sum_mat, mask, same_head


def _hg_kernel(x_ref, lbl_ref, ng_ref, sum_ref, mask_ref, same_ref, o_ref, st_ref, *, layer):
    t = HG_CHUNK
    w = GROUP_W

    @pl.when(pl.program_id(1) == 0)
    def _():
        st_ref[...] = jnp.zeros(st_ref.shape, F32)

    q = x_ref[:, 0:w]
    fz = x_ref[:, w:2 * w]
    v = x_ref[:, 2 * w:3 * w]
    g = x_ref[:, 3 * w:4 * w]

    logits = lbl_ref[...]
    ex = jnp.exp(logits - jnp.max(logits, axis=0, keepdims=True))
    prob = ex / jnp.sum(ex, axis=0, keepdims=True)
    csum = prob[0:1, :]
    for i in range(1, layer + 1):
        csum = csum + prob[i:i + 1, :]
    lb = csum - prob[0:1, :]

    a_ = jnp.log(jnp.maximum(lb, LB_FLOOR))
    b_ = jnp.log1p(-lb) + _log_sigmoid(fz)
    log_f = jnp.maximum(a_, b_) + jnp.log1p(jnp.exp(-jnp.abs(a_ - b_)))
    key = (1.0 - lb) * _sigmoid(-fz)

    hi, lo = _split_bf16(log_f)
    sums = _dot(sum_ref[...], hi) + _dot(sum_ref[...], lo)
    cum = sums[0:t]
    last = cum[t - 1:t, :]

    st = st_ref[...]
    out = _dot_nt((q * jnp.exp(cum)).astype(BF16), st.astype(BF16))
    vb = v.astype(BF16)
    kdec = (key * jnp.exp(last - cum)).astype(BF16)
    same = same_ref[...]
    st_ref[...] = st * jnp.exp(last) + jnp.where(same > 0, _dot_tn(vb, kdec), 0.0)

    lane_head = lax.broadcasted_iota(jnp.int32, (t, w), 1) // HEAD_DIM
    scores = [jnp.zeros((t, t), F32) for _ in range(N_HEADS)]
    for lev in range(HG_LEVELS + 1):
        if lev == 0:
            qt, kt = q, key
        else:
            decay = jnp.exp(sums[lev * t:(lev + 1) * t])
            qt, kt = q * decay, key * decay
        kt = kt.astype(BF16)
        for hh in range(N_HEADS):
            qh = jnp.where(lane_head == hh, qt, 0.0).astype(BF16)
            scores[hh] = scores[hh] + _dot_nt(qh, kt) * mask_ref[lev]
    for hh in range(N_HEADS):
        out = out + jnp.where(lane_head == hh, _dot(scores[hh].astype(BF16), vb), 0.0)

    sq_hi, sq_lo = _split_bf16(out * out)
    mean_sq = (_dot(sq_hi, same) + _dot(sq_lo, same)) * (1.0 / HEAD_DIM)
    normed = out * lax.rsqrt(mean_sq + NORM_EPS) * ng_ref[...]
    o_ref[...] = (normed * (g * _sigmoid(g))).astype(BF16)


def _hgrn2(x, lb_logits, ng, layer, batch):
    m = x.shape[0]
    t = HG_CHUNK
    nc = m // batch // t
    sum_mat, mask, same_head = _hg_constants()
    full = lambda a: pl.BlockSpec(a.shape, lambda b, c: (0,) * a.ndim)
    return pl.pallas_call(
        functools.partial(_hg_kernel, layer=layer),
        out_shape=jax.ShapeDtypeStruct((m, GROUP_W), BF16),
        grid=(batch, nc),
        in_specs=[pl.BlockSpec((t, 4 * GROUP_W), lambda b, c: (b * nc + c, 0)),
                  full(lb_logits), full(ng), full(sum_mat), full(mask), full(same_head)],
        out_specs=pl.BlockSpec((t, GROUP_W), lambda b, c: (b * nc + c, 0)),
        scratch_shapes=[pltpu.VMEM((GROUP_W, GROUP_W), F32)],
        compiler_params=_params("arbitrary", "arbitrary"),
        name="hgrn2",
    )(x, lb_logits, ng, sum_mat, mask, same_head)


def _out_ffn_kernel(h_ref, ya_ref, yb_ref, yc_ref, yd_ref, wout_ref, g2_ref, wg_ref, wu_ref, wd_ref,
                    gf_ref, o_ref, *, final):
    mixed = jnp.concatenate([ya_ref[...], yb_ref[...], yc_ref[...], yd_ref[...]], axis=1)
    h1 = h_ref[...] + _dot(mixed, wout_ref[...])
    un = _rms(h1, g2_ref[...]).astype(BF16)
    acc = h1
    for c in range(FFN_HIDDEN // FFN_CHUNK):
        sl = slice(c * FFN_CHUNK, (c + 1) * FFN_CHUNK)
        gt = _dot(un, wg_ref[:, sl])
        up = _dot(un, wu_ref[:, sl])
        act = (gt * _sigmoid(gt) * up).astype(BF16)
        acc = acc + _dot(act, wd_ref[sl, :])
    if final:
        acc = _rms(acc, gf_ref[...])
    o_ref[...] = acc


def _out_ffn(h, ya, yb, yc, yd, wout, g2, wg, wu, wd, gf, final):
    m = h.shape[0]
    tm = ROW_TILE
    rows = lambda w: pl.BlockSpec((tm, w), lambda i: (i, 0))
    once = lambda a: pl.BlockSpec(a.shape, lambda i: (0, 0), pipeline_mode=pl.Buffered(1))
    return pl.pallas_call(
        functools.partial(_out_ffn_kernel, final=final),
        out_shape=jax.ShapeDtypeStruct((m, D_MODEL), F32),
        grid=(m // tm,),
        in_specs=[rows(D_MODEL), rows(GROUP_W), rows(GROUP_W), rows(GROUP_W), rows(GROUP_W),
                  once(wout), once(g2), once(wg), once(wu), once(wd), once(gf)],
        out_specs=rows(D_MODEL),
        compiler_params=_params("arbitrary"),
        name="out_ffn",
    )(h, ya, yb, yc, yd, wout, g2, wg, wu, wd, gf)


def _pad_in_proj(w):
    d = w.shape[0]
    z = lambda n: jnp.zeros((d, n), w.dtype)
    sb_q = w[:, 928:928 + GROUP_W] * (LOG2_E * HEAD_DIM ** -0.5)
    return jnp.concatenate([w[:, :384], z(MLA_NOPE), w[:, 384:416], z(LANES - MLA_NOPE - MLA_ROPE),
                            w[:, 416:928], sb_q, w[:, 928 + GROUP_W:]], axis=1).astype(BF16)


def _pad_heads(w, width):
    k = w.shape[0]
    w = w.reshape(k, N_HEADS, width)
    w = jnp.pad(w, ((0, 0), (0, 0), (0, LANES - width)))
    return w.reshape(k, N_HEADS * LANES)


def _block_diag(w):
    n, c, d = w.shape
    eye = jnp.eye(n, dtype=w.dtype)
    return (eye[:, None, :, None] * w[:, :, None, :]).reshape(n * c, n * d)


def kernel(x, positions, ln1_g, w_in, mla_q_norm_g, mla_w_uq, mla_kv_norm_g, mla_w_ukv, lru_conv_w, lru_conv_b, lru_w_a, lru_b_a, lru_w_x, lru_b_x, lru_lambda, hgrn_lb_logits, hgrn_norm_g, group_norm_g, w_out, ln2_g, w_ffn_gate, w_ffn_up, w_ffn_down, final_norm_g):
    batch, seq, d_model = x.shape
    depth = w_in.shape[0]
    m = batch * seq
    row = lambda v: v.reshape(1, -1)
    h = x.reshape(m, d_model)
    tabs = _rope_tables(positions)
    qk_scale = LOG2_E * (MLA_NOPE + MLA_ROPE) ** -0.5
    for l in range(depth):
        wuq = _pad_heads(mla_w_uq[l] * qk_scale, MLA_NOPE + MLA_ROPE).astype(BF16)
        wukv = mla_w_ukv[l].reshape(MLA_KV_LORA, N_HEADS, MLA_NOPE + MLA_V)
        wuk = _pad_heads(wukv[:, :, :MLA_NOPE].reshape(MLA_KV_LORA, -1), MLA_NOPE)
        wuv = wukv[:, :, MLA_NOPE:].reshape(MLA_KV_LORA, -1)
        wukv_p = jnp.concatenate([wuk, wuv], axis=1).astype(BF16)
        qm, km, vm_t, lru_in, sb_qk, sb_vt, hg_in = _in_proj(
            h, row(ln1_g[l]), _pad_in_proj(w_in[l]), row(mla_q_norm_g[l]), wuq,
            row(mla_kv_norm_g[l]), wukv_p, tabs)

        y_a = _mla_attention(qm, km, vm_t, row(group_norm_g[l, 0]), batch)

        to_time_major = lambda a: a.reshape(batch, seq, LRU_WIDTH).transpose(1, 0, 2)
        wab = jnp.concatenate([_block_diag(lru_w_a[l]), _block_diag(lru_w_x[l])], axis=1).astype(BF16)
        y_b = _rglru(to_time_major(lru_in[:, :LRU_WIDTH]), to_time_major(lru_in[:, LRU_WIDTH:]),
                     lru_conv_w[l], row(lru_conv_b[l]), wab, row(lru_b_a[l]), row(lru_b_x[l]),
                     row(lru_lambda[l]), row(group_norm_g[l, 1]))
        y_b = y_b.transpose(1, 0, 2).reshape(m, LRU_WIDTH).astype(BF16)

        y_c = _sb_attention(sb_qk, sb_vt, row(group_norm_g[l, 2]), batch)
        y_d = _hgrn2(hg_in, hgrn_lb_logits, row(hgrn_norm_g[l]), l, batch)

        h = _out_ffn(h, y_a, y_b, y_c, y_d, w_out[l].astype(BF16), row(ln2_g[l]),
                     w_ffn_gate[l].astype(BF16), w_ffn_up[l].astype(BF16), w_ffn_down[l].astype(BF16),
                     row(final_norm_g), final=(l == depth - 1))
    return h.reshape(batch, seq, d_model)
```

```python
import functools
import math

import numpy as np
import jax
import jax.numpy as jnp
from jax import lax
from jax.experimental import pallas as pl
from jax.experimental.pallas import tpu as pltpu

F32 = jnp.float32
BF16 = jnp.bfloat16

D_MODEL = 1024
N_HEADS = 4
MLA_Q_LORA = 256
MLA_KV_LORA = 128
MLA_NOPE = 64
MLA_ROPE = 32
MLA_V = 64
ROPE_THETA = 10000.0
LRU_WIDTH = 256
CONV_WIDTH = 4
LRU_C = 8.0
HEAD_DIM = 64
GROUP_W = 256
FFN_HIDDEN = 2816
NORM_EPS = 1e-6
MASK_VALUE = -1e30
LB_FLOOR = 1e-30

LANES = 128
VMEM_LIMIT_BYTES = 56 * 1024 * 1024

ROW_TILE = 512
ATTN_TILE = 256
SOFTMAX_ROWS = 64
SUM_ROWS = 16
PIPE_DEPTH = 2
SB_BODY_STEPS = 4
LOG2_E = math.log2(math.e)
LRU_TIME_TILE = 256
HG_CHUNK = 128
HG_LEVELS = int(math.log2(HG_CHUNK))
HG_SEQS = 4
FFN_CHUNK = 1408

IN_MLA = (0, 512)
IN_LRU = (512, 1024)
IN_SB = (1024, 1792)
IN_HG = (1792, 2816)
D_IN_PAD = 2816


def _params(*semantics):
    return pltpu.CompilerParams(dimension_semantics=semantics, vmem_limit_bytes=VMEM_LIMIT_BYTES)


def _rms(x, g):
    return x * lax.rsqrt(jnp.mean(x * x, axis=-1, keepdims=True) + NORM_EPS) * g


def _dot(a, b):
    return jnp.dot(a, b, preferred_element_type=F32)


def _dot_nt(a, b):
    return lax.dot_general(a, b, (((1,), (1,)), ((), ())), preferred_element_type=F32)


def _dot_tn(a, b):
    return lax.dot_general(a, b, (((0,), (0,)), ((), ())), preferred_element_type=F32)


def _split_bf16(x):
    hi = x.astype(BF16)
    lo = (x - hi.astype(F32)).astype(BF16)
    return hi, lo


def _sigmoid(x):
    return 1.0 / (1.0 + jnp.exp(-x))


def _log_sigmoid(x):
    return jnp.minimum(x, 0.0) - jnp.log1p(jnp.exp(-jnp.abs(x)))


def _rope_table_kernel(pos_ref, freq_ref, m1_ref, m2_ref, c_ref, s1_ref, s2_ref):
    ang = pos_ref[...] * freq_ref[...]
    sn = jnp.sin(ang)
    c_ref[...] = jnp.cos(ang)
    s1_ref[...] = sn * m1_ref[...]
    s2_ref[...] = sn * m2_ref[...]


def _rope_tables(positions):
    m = positions.size
    half = MLA_ROPE // 2
    inv_freq = ROPE_THETA ** (-jnp.arange(half, dtype=F32) / half)
    zeros = jnp.zeros((MLA_NOPE,), F32)
    tail = jnp.zeros((LANES - MLA_NOPE - MLA_ROPE,), F32)
    freq = jnp.concatenate([zeros, inv_freq, inv_freq, tail]).reshape(1, LANES)
    ones = jnp.ones((half,), F32)
    m1 = jnp.concatenate([zeros, -ones, 0 * ones, tail]).reshape(1, LANES)
    m2 = jnp.concatenate([zeros, 0 * ones, ones, tail]).reshape(1, LANES)
    pos = positions.astype(F32).reshape(m, 1)
    tm = ROW_TILE
    row = pl.BlockSpec((1, LANES), lambda i: (0, 0))
    tab = pl.BlockSpec((tm, LANES), lambda i: (i, 0))
    shape = jax.ShapeDtypeStruct((m, LANES), F32)
    return pl.pallas_call(
        _rope_table_kernel,
        out_shape=(shape, shape, shape),
        grid=(m // tm,),
        in_specs=[pl.BlockSpec((tm, 1), lambda i: (i, 0)), row, row, row],
        out_specs=(tab, tab, tab),
        compiler_params=_params("arbitrary"),
        name="rope_tables",
    )(pos, freq, m1, m2)


def _in_proj_kernel(h_ref, g1_ref, win_ref, qg_ref, wuq_ref, kvg_ref, wukv_ref,
                    c_ref, s1_ref, s2_ref,
                    qm_ref, km_ref, vmt_ref, lru_ref, sb_ref, sbvt_ref, hg_ref):
    t = ATTN_TILE
    n_blocks = h_ref.shape[0] // t

    def store_transposed(ref, v):
        for blk in range(n_blocks):
            ref[blk] = v[blk * t:(blk + 1) * t, :].T.astype(BF16)

    u = _rms(h_ref[...], g1_ref[...]).astype(BF16)
    lru_ref[...] = _dot(u, win_ref[:, IN_LRU[0]:IN_LRU[1]])
    sb = _dot(u, win_ref[:, IN_SB[0]:IN_SB[1]])
    sb_ref[...] = sb[:, :2 * GROUP_W].astype(BF16)
    store_transposed(sbvt_ref, sb[:, 2 * GROUP_W:])
    hg_ref[...] = _dot(u, win_ref[:, IN_HG[0]:IN_HG[1]])

    p = _dot(u, win_ref[:, IN_MLA[0]:IN_MLA[1]])
    cq = _rms(p[:, 0:MLA_Q_LORA], qg_ref[...]).astype(BF16)
    ckv = _rms(p[:, MLA_Q_LORA:MLA_Q_LORA + MLA_KV_LORA], kvg_ref[...]).astype(BF16)
    q = _dot(cq, wuq_ref[...])
    kv = _dot(ckv, wukv_ref[...])
    c, s1, s2 = c_ref[...], s1_ref[...], s2_ref[...]

    def rope(x):
        return x * c + pltpu.roll(x, LANES - MLA_ROPE // 2, 1) * s1 + pltpu.roll(x, MLA_ROPE // 2, 1) * s2

    kpe = rope(p[:, 384:512])
    for hh in range(N_HEADS):
        sl = slice(hh * LANES, (hh + 1) * LANES)
        qm_ref[:, sl] = rope(q[:, sl]).astype(BF16)
        km_ref[:, sl] = (kv[:, sl] + kpe).astype(BF16)
    store_transposed(vmt_ref, kv[:, N_HEADS * LANES:])


def _in_proj(h, g1, win, qg, wuq, kvg, wukv, tabs):
    m = h.shape[0]
    tm = ROW_TILE
    t = ATTN_TILE
    rows = lambda w: pl.BlockSpec((tm, w), lambda i: (i, 0))
    full = lambda a: pl.BlockSpec(a.shape, lambda i: (0, 0))
    c, s1, s2 = tabs
    v_t = jax.ShapeDtypeStruct((m // t, GROUP_W, t), BF16)
    out_shape = (
        jax.ShapeDtypeStruct((m, N_HEADS * LANES), BF16),
        jax.ShapeDtypeStruct((m, N_HEADS * LANES), BF16),
        v_t,
        jax.ShapeDtypeStruct((m, 2 * LRU_WIDTH), F32),
        jax.ShapeDtypeStruct((m, 2 * GROUP_W), BF16),
        v_t,
        jax.ShapeDtypeStruct((m, 4 * GROUP_W), F32),
    )

    def out_spec(sd):
        if len(sd.shape) == 3:
            return pl.BlockSpec((tm // t, GROUP_W, t), lambda i: (i, 0, 0))
        return rows(sd.shape[1])
    return pl.pallas_call(
        _in_proj_kernel,
        out_shape=out_shape,
        grid=(m // tm,),
        in_specs=[rows(D_MODEL), full(g1), full(win), full(qg), full(wuq), full(kvg), full(wukv),
                  rows(LANES), rows(LANES), rows(LANES)],
        out_specs=tuple(out_spec(sd) for sd in out_shape),
        compiler_params=_params("arbitrary"),
        name="in_proj",
    )(h, g1, win, qg, wuq, kvg, wukv, c, s1, s2)


def _mla_kernel(q_ref, k_ref, vt_ref, gn_ref, o_ref, m_ref, acc_ref, s0_ref, s1_ref, p_ref):
    t = ATTN_TILE
    qi = pl.program_id(1)
    m_ref[...] = jnp.full(m_ref.shape, MASK_VALUE, F32)
    acc_ref[...] = jnp.zeros(acc_ref.shape, F32)
    n_chunks = t // SOFTMAX_ROWS
    key_idx = lax.broadcasted_iota(jnp.int32, (SOFTMAX_ROWS, t), 0)
    qry_idx = lax.broadcasted_iota(jnp.int32, (SOFTMAX_ROWS, t), 1)
    ones_rows = jnp.ones((SUM_ROWS, t), BF16)

    def scores(j, hh, s_ref):
        r0 = pl.multiple_of(j * t, t)
        sl = slice(hh * LANES, (hh + 1) * LANES)
        s_ref[hh] = _dot_nt(k_ref[pl.ds(r0, t), sl], q_ref[:, sl])

    def softmax_pv(j, hh, s_ref, diagonal):
        def chunk(c):
            s = s_ref[hh, c * SOFTMAX_ROWS:(c + 1) * SOFTMAX_ROWS, :]
            if diagonal:
                s = jnp.where(key_idx + c * SOFTMAX_ROWS <= qry_idx, s, MASK_VALUE)
            return s

        m_prev = m_ref[hh]
        m_blk = chunk(0)
        for c in range(1, n_chunks):
            m_blk = jnp.maximum(m_blk, chunk(c))
        m_new = jnp.maximum(m_prev, jnp.max(m_blk, axis=0, keepdims=True))
        alpha = jnp.exp2(m_prev - m_new)
        for c in range(n_chunks):
            p_ref[hh, c * SOFTMAX_ROWS:(c + 1) * SOFTMAX_ROWS, :] = jnp.exp2(chunk(c) - m_new).astype(BF16)
        v_ext = jnp.concatenate([vt_ref[j, hh * MLA_V:(hh + 1) * MLA_V, :], ones_rows], axis=0)
        acc_ref[hh] = alpha * acc_ref[hh] + _dot(v_ext, p_ref[hh])
        m_ref[hh] = m_new

    def stage(j_next, next_ref, j_cur, cur_ref, diagonal=False):
        for hh in range(N_HEADS):
            if next_ref is not None:
                scores(j_next, hh, next_ref)
            if cur_ref is not None:
                softmax_pv(j_cur, hh, cur_ref, diagonal)

    stage(0, s0_ref, None, None)

    def body(pair, carry):
        j = 2 * pair
        stage(j + 1, s1_ref, j, s0_ref)
        stage(j + 2, s0_ref, j + 1, s1_ref)
        return carry

    lax.fori_loop(0, qi // 2, body, 0)

    @pl.when(qi % 2 == 0)
    def _():
        stage(None, None, qi, s0_ref, True)

    @pl.when(qi % 2 == 1)
    def _():
        stage(qi, s1_ref, qi - 1, s0_ref)
        stage(None, None, qi, s1_ref, True)

    y_t = jnp.concatenate([acc_ref[hh, 0:MLA_V, :] / acc_ref[hh, MLA_V:MLA_V + 1, :] for hh in range(N_HEADS)],
                          axis=0)
    o_ref[...] = _rms(y_t.T, gn_ref[...]).astype(BF16)


def _mla_attention(q, k, v_t, gn, batch):
    m = q.shape[0]
    s = m // batch
    t = ATTN_TILE
    nq = s // t
    return pl.pallas_call(
        _mla_kernel,
        out_shape=jax.ShapeDtypeStruct((m, GROUP_W), BF16),
        grid=(batch, nq),
        in_specs=[pl.BlockSpec((t, N_HEADS * LANES), lambda b, i: (b * nq + i, 0)),
                  pl.BlockSpec((s, N_HEADS * LANES), lambda b, i: (b, 0)),
                  pl.BlockSpec((nq, GROUP_W, t), lambda b, i: (b, 0, 0)),
                  pl.BlockSpec((1, GROUP_W), lambda b, i: (0, 0))],
        out_specs=pl.BlockSpec((t, GROUP_W), lambda b, i: (b * nq + i, 0)),
        scratch_shapes=[pltpu.VMEM((N_HEADS, 1, t), F32),
                        pltpu.VMEM((N_HEADS, MLA_V + SUM_ROWS, t), F32),
                        pltpu.VMEM((N_HEADS, t, t), F32),
                        pltpu.VMEM((N_HEADS, t, t), F32),
                        pltpu.VMEM((N_HEADS, t, t), BF16)],
        compiler_params=_params("arbitrary", "arbitrary"),
        name="mla_attention",
    )(q, k, v_t, gn)


def _sb_kernel(q_ref, k_ref, vt_ref, tri_ref, gn_ref, o_ref,
               qm_ref, carry_ref, acc_ref, z0_ref, z1_ref, lb_ref, lk_ref, later_ref, w_ref):
    t = ATTN_TILE
    qi = pl.program_id(1)
    n_chunks = t // SOFTMAX_ROWS
    lane = lax.broadcasted_iota(jnp.int32, (t, LANES), 1)
    for hh in range(N_HEADS):
        qp = q_ref[:, (hh // 2) * LANES:(hh // 2 + 1) * LANES]
        own = (lane >= (hh % 2) * HEAD_DIM) & (lane < (hh % 2 + 1) * HEAD_DIM)
        qm_ref[hh] = jnp.where(own, qp, jnp.zeros_like(qp))
    carry_ref[...] = jnp.zeros(carry_ref.shape, F32)
    acc_ref[...] = jnp.zeros(acc_ref.shape, F32)
    key_idx = lax.broadcasted_iota(jnp.int32, (SOFTMAX_ROWS, t), 0)
    qry_idx = lax.broadcasted_iota(jnp.int32, (SOFTMAX_ROWS, t), 1)

    def rows(c):
        return slice(c * SOFTMAX_ROWS, (c + 1) * SOFTMAX_ROWS)

    def logits(step, hh, z_ref):
        r0 = pl.multiple_of(jnp.maximum(qi - step, 0) * t, t)
        ps = slice((hh // 2) * LANES, (hh // 2 + 1) * LANES)
        z_ref[hh] = _dot_nt(k_ref[pl.ds(r0, t), ps], qm_ref[hh])

    def log_terms(hh, z_ref, diagonal):
        for c in range(n_chunks):
            z = z_ref[hh, rows(c), :]
            softplus = jnp.log2(1.0 + jnp.exp2(-jnp.abs(z)))
            log_beta = jnp.minimum(z, 0.0) - softplus
            log_keep = log_beta - z
            if diagonal:
                log_keep = jnp.where(key_idx + c * SOFTMAX_ROWS < qry_idx, log_keep, 0.0)
            lb_ref[hh, rows(c), :] = log_beta
            lk_ref[hh, rows(c), :] = log_keep.astype(BF16)
        later_ref[hh] = _dot(tri_ref[...], lk_ref[hh])

    def weights_pv(step, hh, diagonal):
        carry = carry_ref[hh]
        if diagonal is None:
            qry_lim = qry_idx + jnp.where(step == 0, 0, t)
        for c in range(n_chunks):
            w = jnp.exp2(lb_ref[hh, rows(c), :] + later_ref[hh, rows(c), :] + carry)
            if diagonal is None:
                w = jnp.where(key_idx + c * SOFTMAX_ROWS < qry_lim, w, 0.0)
            elif diagonal:
                w = jnp.where(key_idx + c * SOFTMAX_ROWS < qry_idx, w, 0.0)
            w_ref[hh, rows(c), :] = w.astype(BF16)
        carry_ref[hh] = carry + later_ref[hh, t:t + 1, :]
        v_t = vt_ref[qi - step, hh * HEAD_DIM:(hh + 1) * HEAD_DIM, :]
        acc_ref[hh] += _dot(v_t, w_ref[hh])

    tail_heads = range(N_HEADS - PIPE_DEPTH, N_HEADS)

    def run(first_step, n_steps, diagonal=False, resume=True):
        z_refs = (z0_ref, z1_ref)
        pending = [(first_step - 1, hh, None) for hh in tail_heads] if resume else []
        for i in range(n_steps):
            step = first_step + i
            parity = i % 2 if diagonal else (i + 1) % 2
            for hh in range(N_HEADS):
                ahead = hh + PIPE_DEPTH
                if ahead < N_HEADS:
                    logits(step, ahead, z_refs[parity])
                else:
                    logits(step + 1, ahead - N_HEADS, z_refs[1 - parity])
                log_terms(hh, z_refs[parity], diagonal)
                pending.append((step, hh, diagonal))
                if len(pending) > PIPE_DEPTH:
                    weights_pv(*pending.pop(0))

    def finish(last_step, diagonal):
        for hh in tail_heads:
            weights_pv(last_step, hh, diagonal)

    for hh in range(PIPE_DEPTH):
        logits(0, hh, z0_ref)
    run(0, 1, diagonal=True, resume=False)

    def body(i, carry):
        run(1 + SB_BODY_STEPS * i, SB_BODY_STEPS)
        return carry

    n_body = qi // SB_BODY_STEPS
    lax.fori_loop(0, n_body, body, 0)
    done = n_body * SB_BODY_STEPS
    left = qi - done

    @pl.when(left >= 2)
    def _():
        run(done + 1, 2)

    done2 = done + 2 * (left // 2)

    @pl.when(left % 2 == 1)
    def _():
        run(done2 + 1, 1)

    finish(qi, None)

    y_t = jnp.concatenate([acc_ref[hh] for hh in range(N_HEADS)], axis=0)
    o_ref[...] = _rms(y_t.T, gn_ref[...]).astype(BF16)


def _sb_attention(qk, v_t, gn, batch):
    m = qk.shape[0]
    s = m // batch
    t = ATTN_TILE
    nq = s // t
    idx = np.arange(t)
    later = idx[None, :] > idx[:, None]
    total = np.ones((SUM_ROWS, t), bool)
    tri = jnp.asarray(np.concatenate([later, total], axis=0), BF16)
    return pl.pallas_call(
        _sb_kernel,
        out_shape=jax.ShapeDtypeStruct((m, GROUP_W), BF16),
        grid=(batch, nq),
        in_specs=[pl.BlockSpec((t, GROUP_W), lambda b, i: (b * nq + i, 0)),
                  pl.BlockSpec((s, GROUP_W), lambda b, i: (b, 1)),
                  pl.BlockSpec((nq, GROUP_W, t), lambda b, i: (b, 0, 0)),
                  pl.BlockSpec(tri.shape, lambda b, i: (0, 0)),
                  pl.BlockSpec((1, GROUP_W), lambda b, i: (0, 0))],
        out_specs=pl.BlockSpec((t, GROUP_W), lambda b, i: (b * nq + i, 0)),
        scratch_shapes=[pltpu.VMEM((N_HEADS, t, LANES), BF16),
                        pltpu.VMEM((N_HEADS, 1, t), F32),
                        pltpu.VMEM((N_HEADS, HEAD_DIM, t), F32),
                        pltpu.VMEM((N_HEADS, t, t), F32),
                        pltpu.VMEM((N_HEADS, t, t), F32),
                        pltpu.VMEM((N_HEADS, t, t), F32),
                        pltpu.VMEM((N_HEADS, t, t), BF16),
                        pltpu.VMEM((N_HEADS, t + SUM_ROWS, t), F32),
                        pltpu.VMEM((N_HEADS, t, t), BF16)],
        compiler_params=_params("arbitrary", "arbitrary"),
        name="sb_attention",
    )(qk, qk, v_t, tri, gn)


def _lru_kernel(x_ref, gate_ref, cw_ref, cb_ref, wab_ref, ba_ref, bx_ref, lam_ref, gn_ref,
                o_ref, xs_ref, a_ref, u_ref, h_ref):
    ts, nb, w = x_ref.shape
    pad = CONV_WIDTH - 1

    @pl.when(pl.program_id(0) == 0)
    def _():
        xs_ref[0:pad] = jnp.zeros((pad, nb, w), F32)
        h_ref[...] = jnp.zeros(h_ref.shape, F32)

    @pl.when(pl.program_id(0) > 0)
    def _():
        xs_ref[0:pad] = xs_ref[ts:ts + pad]

    xs_ref[pad:pad + ts] = x_ref[...]
    xc = cb_ref[...].reshape(1, 1, w)
    for i in range(CONV_WIDTH):
        xc = xc + cw_ref[i:i + 1, :].reshape(1, 1, w) * xs_ref[i:i + ts]
    flat = xc.reshape(ts * nb, w)
    gates = _dot(flat.astype(BF16), wab_ref[...])
    rec_gate = _sigmoid(gates[:, :w] + ba_ref[...])
    in_gate = _sigmoid(gates[:, w:] + bx_ref[...])
    lam = lam_ref[...]
    softplus_neg_lam = jnp.maximum(-lam, 0.0) + jnp.log1p(jnp.exp(-jnp.abs(lam)))
    log_a = -LRU_C * rec_gate * softplus_neg_lam
    a = jnp.exp(log_a)
    a_ref[...] = a.reshape(ts, nb, w)
    one_minus_a2 = -jnp.tanh(log_a) * (a * a + 1.0)
    u_ref[...] = (jnp.sqrt(jnp.maximum(one_minus_a2, 0.0)) * (in_gate * flat)).reshape(ts, nb, w)

    def step(i, h):
        h = a_ref[i] * h + u_ref[i]
        u_ref[i] = h
        return h

    h_ref[...] = lax.fori_loop(0, ts, step, h_ref[...], unroll=8)

    g = gate_ref[...]
    gelu = 0.5 * g * (1.0 + jnp.tanh(math.sqrt(2.0 / math.pi) * (g + 0.044715 * (g * g * g))))
    o_ref[...] = _rms(u_ref[...] * gelu, gn_ref[...].reshape(1, 1, w))


def _rglru(x_t, gate_t, cw, cb, wab, ba, bx, lam, gn):
    s, nb, w = x_t.shape
    ts = LRU_TIME_TILE
    tile = pl.BlockSpec((ts, nb, w), lambda i: (i, 0, 0))
    full = lambda a: pl.BlockSpec(a.shape, lambda i: (0,) * a.ndim)
    return pl.pallas_call(
        _lru_kernel,
        out_shape=jax.ShapeDtypeStruct((s, nb, w), F32),
        grid=(s // ts,),
        in_specs=[tile, tile, full(cw), full(cb), full(wab), full(ba), full(bx), full(lam), full(gn)],
        out_specs=tile,
        scratch_shapes=[pltpu.VMEM((ts + CONV_WIDTH - 1, nb, w), F32),
                        pltpu.VMEM((ts, nb, w), F32),
                        pltpu.VMEM((ts, nb, w), F32),
                        pltpu.VMEM((nb, w), F32)],
        compiler_params=_params("arbitrary"),
        name="rglru",
    )(x_t, gate_t, cw, cb, wab, ba, bx, lam, gn)


def _hg_constants():
    t = HG_CHUNK
    idx = np.arange(t)
    sums = [idx[:, None] >= idx[None, :]]
    masks = [idx[:, None] == idx[None, :]]
    for lev in range(HG_LEVELS):
        half = 1 << lev
        mid = (idx // (2 * half)) * (2 * half) + half
        upper = idx >= mid
        j = idx[None, :]
        up_rows = upper[:, None] & (j >= mid[:, None]) & (j <= idx[:, None])
        low_rows = (~upper)[:, None] & (j > idx[:, None]) & (j < mid[:, None])
        sums.append(up_rows | low_rows)
        same = (idx[:, None] // (2 * half)) == (idx[None, :] // (2 * half))
        masks.append(same & upper[:, None] & (~upper)[None, :])
    sum_mat = np.concatenate(sums, axis=0)
    sum_mat = jnp.asarray(np.concatenate([sum_mat, sum_mat], axis=1), BF16)
    mask = jnp.asarray(np.stack(masks, axis=0), F32)
    head = np.arange(GROUP_W) // HEAD_DIM
    same_head = head[:, None] == head[None, :]
    same_head = jnp.asarray(np.concatenate([same_head, same_head], axis=0), BF16)
    return sum_mat, mask, same_head


def _hg_kernel(x_ref, lbl_ref, ng_ref, sum_ref, mask_ref, same_ref, o_ref, st_ref, *, layer):
    t = HG_CHUNK
    w = GROUP_W

    seqs = range(x_ref.shape[0])

    @pl.when(pl.program_id(1) == 0)
    def _():
        st_ref[...] = jnp.zeros(st_ref.shape, F32)

    logits = lbl_ref[...]
    ex = jnp.exp(logits - jnp.max(logits, axis=0, keepdims=True))
    prob = ex / jnp.sum(ex, axis=0, keepdims=True)
    csum = prob[0:1, :]
    for i in range(1, layer + 1):
        csum = csum + prob[i:i + 1, :]
    lb = csum - prob[0:1, :]
    a_ = jnp.log(jnp.maximum(lb, LB_FLOOR))
    same = same_ref[...]
    lane_head = lax.broadcasted_iota(jnp.int32, (t, w), 1) // HEAD_DIM

    def by_head(x):
        return jnp.concatenate([jnp.where(lane_head == hh, x, jnp.zeros_like(x)) for hh in range(N_HEADS)], axis=0)

    q = [x_ref[b, :, 0:w] for b in seqs]
    vb = [x_ref[b, :, 2 * w:3 * w].astype(BF16) for b in seqs]
    key, sums = [], []
    for b in seqs:
        fz = x_ref[b, :, w:2 * w]
        b_ = jnp.log1p(-lb) + _log_sigmoid(fz)
        log_f = jnp.maximum(a_, b_) + jnp.log1p(jnp.exp(-jnp.abs(a_ - b_)))
        key.append((1.0 - lb) * _sigmoid(-fz))
        sums.append(_dot(sum_ref[...], jnp.concatenate(_split_bf16(log_f), axis=0)))

    out = []
    for b in seqs:
        cum = sums[b][0:t]
        last = cum[t - 1:t, :]
        st = st_ref[b]
        out.append(_dot_nt((q[b] * jnp.exp(cum)).astype(BF16), st.astype(BF16)))
        kdec = (key[b] * jnp.exp(last - cum)).astype(BF16)
        st_ref[b] = st * jnp.exp(last) + jnp.where(same[0:w] > 0, _dot_tn(vb[b], kdec), 0.0)

    scores = [[jnp.zeros((t, t), F32) for _ in range(N_HEADS)] for _ in seqs]
    for lev in range(HG_LEVELS + 1):
        for b in seqs:
            if lev == 0:
                qt, kt = q[b], key[b]
            else:
                decay = jnp.exp(sums[b][lev * t:(lev + 1) * t])
                qt, kt = q[b] * decay, key[b] * decay
            sc = _dot_nt(qt.astype(BF16), by_head(kt.astype(BF16)))
            for hh in range(N_HEADS):
                scores[b][hh] = scores[b][hh] + sc[:, hh * t:(hh + 1) * t] * mask_ref[lev]
    for b in seqs:
        all_scores = jnp.concatenate([s.astype(BF16) for s in scores[b]], axis=1)
        out[b] = out[b] + _dot(all_scores, by_head(vb[b]))

    mean_sq = [_dot(jnp.concatenate(_split_bf16(out[b] * out[b]), axis=1), same) * (1.0 / HEAD_DIM) for b in seqs]
    for b in seqs:
        g = x_ref[b, :, 3 * w:4 * w]
        normed = out[b] * lax.rsqrt(mean_sq[b] + NORM_EPS) * ng_ref[...]
        o_ref[b] = (normed * (g * _sigmoid(g))).astype(BF16)


def _hgrn2(x, lb_logits, ng, layer, batch):
    m = x.shape[0]
    t = HG_CHUNK
    nb = HG_SEQS
    seq = m // batch
    sum_mat, mask, same_head = _hg_constants()
    full = lambda a: pl.BlockSpec(a.shape, lambda b, c: (0,) * a.ndim)
    y = pl.pallas_call(
        functools.partial(_hg_kernel, layer=layer),
        out_shape=jax.ShapeDtypeStruct((batch, seq, GROUP_W), BF16),
        grid=(batch // nb, seq // t),
        in_specs=[pl.BlockSpec((nb, t, 4 * GROUP_W), lambda b, c: (b, c, 0)),
                  full(lb_logits), full(ng), full(sum_mat), full(mask), full(same_head)],
        out_specs=pl.BlockSpec((nb, t, GROUP_W), lambda b, c: (b, c, 0)),
        scratch_shapes=[pltpu.VMEM((nb, GROUP_W, GROUP_W), F32)],
        compiler_params=_params("arbitrary", "arbitrary"),
        name="hgrn2",
    )(x.reshape(batch, seq, 4 * GROUP_W), lb_logits, ng, sum_mat, mask, same_head)
    return y.reshape(m, GROUP_W)


def _out_ffn_kernel(h_ref, ya_ref, yb_ref, yc_ref, yd_ref, wout_ref, g2_ref, wg_ref, wu_ref, wd_ref,
                    gf_ref, o_ref, *, final):
    mixed = jnp.concatenate([ya_ref[...], yb_ref[...], yc_ref[...], yd_ref[...]], axis=1)
    h1 = h_ref[...] + _dot(mixed, wout_ref[...])
    un = _rms(h1, g2_ref[...]).astype(BF16)
    acc = h1
    for c in range(FFN_HIDDEN // FFN_CHUNK):
        sl = slice(c * FFN_CHUNK, (c + 1) * FFN_CHUNK)
        gt = _dot(un, wg_ref[:, sl])
        up = _dot(un, wu_ref[:, sl])
        act = (gt * _sigmoid(gt) * up).astype(BF16)
        acc = acc + _dot(act, wd_ref[sl, :])
    if final:
        acc = _rms(acc, gf_ref[...])
    o_ref[...] = acc


def _out_ffn(h, ya, yb, yc, yd, wout, g2, wg, wu, wd, gf, final):
    m = h.shape[0]
    tm = ROW_TILE
    rows = lambda w: pl.BlockSpec((tm, w), lambda i: (i, 0))
    once = lambda a: pl.BlockSpec(a.shape, lambda i: (0, 0), pipeline_mode=pl.Buffered(1))
    return pl.pallas_call(
        functools.partial(_out_ffn_kernel, final=final),
        out_shape=jax.ShapeDtypeStruct((m, D_MODEL), F32),
        grid=(m // tm,),
        in_specs=[rows(D_MODEL), rows(GROUP_W), rows(GROUP_W), rows(GROUP_W), rows(GROUP_W),
                  once(wout), once(g2), once(wg), once(wu), once(wd), once(gf)],
        out_specs=rows(D_MODEL),
        compiler_params=_params("arbitrary"),
        name="out_ffn",
    )(h, ya, yb, yc, yd, wout, g2, wg, wu, wd, gf)


def _pad_in_proj(w):
    d = w.shape[0]
    z = lambda n: jnp.zeros((d, n), w.dtype)
    sb_q = w[:, 928:928 + GROUP_W] * (LOG2_E * HEAD_DIM ** -0.5)
    return jnp.concatenate([w[:, :384], z(MLA_NOPE), w[:, 384:416], z(LANES - MLA_NOPE - MLA_ROPE),
                            w[:, 416:928], sb_q, w[:, 928 + GROUP_W:]], axis=1).astype(BF16)


def _pad_heads(w, width):
    k = w.shape[0]
    w = w.reshape(k, N_HEADS, width)
    w = jnp.pad(w, ((0, 0), (0, 0), (0, LANES - width)))
    return w.reshape(k, N_HEADS * LANES)


def _block_diag(w):
    n, c, d = w.shape
    eye = jnp.eye(n, dtype=w.dtype)
    return (eye[:, None, :, None] * w[:, :, None, :]).reshape(n * c, n * d)


def kernel(x, positions, ln1_g, w_in, mla_q_norm_g, mla_w_uq, mla_kv_norm_g, mla_w_ukv, lru_conv_w, lru_conv_b, lru_w_a, lru_b_a, lru_w_x, lru_b_x, lru_lambda, hgrn_lb_logits, hgrn_norm_g, group_norm_g, w_out, ln2_g, w_ffn_gate, w_ffn_up, w_ffn_down, final_norm_g):
    batch, seq, d_model = x.shape
    depth = w_in.shape[0]
    m = batch * seq
    row = lambda v: v.reshape(1, -1)
    h = x.reshape(m, d_model)
    tabs = _rope_tables(positions)
    qk_scale = LOG2_E * (MLA_NOPE + MLA_ROPE) ** -0.5
    for l in range(depth):
        wuq = _pad_heads(mla_w_uq[l] * qk_scale, MLA_NOPE + MLA_ROPE).astype(BF16)
        wukv = mla_w_ukv[l].reshape(MLA_KV_LORA, N_HEADS, MLA_NOPE + MLA_V)
        wuk = _pad_heads(wukv[:, :, :MLA_NOPE].reshape(MLA_KV_LORA, -1), MLA_NOPE)
        wuv = wukv[:, :, MLA_NOPE:].reshape(MLA_KV_LORA, -1)
        wukv_p = jnp.concatenate([wuk, wuv], axis=1).astype(BF16)
        qm, km, vm_t, lru_in, sb_qk, sb_vt, hg_in = _in_proj(
            h, row(ln1_g[l]), _pad_in_proj(w_in[l]), row(mla_q_norm_g[l]), wuq,
            row(mla_kv_norm_g[l]), wukv_p, tabs)

        y_a = _mla_attention(qm, km, vm_t, row(group_norm_g[l, 0]), batch)

        to_time_major = lambda a: a.reshape(batch, seq, LRU_WIDTH).transpose(1, 0, 2)
        wab = jnp.concatenate([_block_diag(lru_w_a[l]), _block_diag(lru_w_x[l])], axis=1).astype(BF16)
        y_b = _rglru(to_time_major(lru_in[:, :LRU_WIDTH]), to_time_major(lru_in[:, LRU_WIDTH:]),
                     lru_conv_w[l], row(lru_conv_b[l]), wab, row(lru_b_a[l]), row(lru_b_x[l]),
                     row(lru_lambda[l]), row(group_norm_g[l, 1]))
        y_b = y_b.transpose(1, 0, 2).reshape(m, LRU_WIDTH).astype(BF16)

        y_c = _sb_attention(sb_qk, sb_vt, row(group_norm_g[l, 2]), batch)
        y_d = _hgrn2(hg_in, hgrn_lb_logits, row(hgrn_norm_g[l]), l, batch)

        h = _out_ffn(h, y_a, y_b, y_c, y_d, w_out[l].astype(BF16), row(ln2_g[l]),
                     w_ffn_gate[l].astype(BF16), w_ffn_up[l].astype(BF16), w_ffn_down[l].astype(BF16),
                     row(final_norm_g), final=(l == depth - 1))
    return h.reshape(batch, seq, d_model)
```

```python
import functools
import math

import numpy as np
import jax
import jax.numpy as jnp
from jax import lax
from jax.experimental import pallas as pl
from jax.experimental.pallas import tpu as pltpu

F32 = jnp.float32
BF16 = jnp.bfloat16

D_MODEL = 1024
N_HEADS = 4
MLA_Q_LORA = 256
MLA_KV_LORA = 128
MLA_NOPE = 64
MLA_ROPE = 32
MLA_V = 64
ROPE_THETA = 10000.0
LRU_WIDTH = 256
CONV_WIDTH = 4
LRU_C = 8.0
HEAD_DIM = 64
GROUP_W = 256
FFN_HIDDEN = 2816
NORM_EPS = 1e-6
MASK_VALUE = -1e30
LB_FLOOR = 1e-30

LANES = 128
VMEM_LIMIT_BYTES = 56 * 1024 * 1024

ROW_TILE = 512
ATTN_TILE = 256
SOFTMAX_ROWS = 64
SUM_ROWS = 16
PIPE_DEPTH = 2
ATTN_BODY_BLOCKS = 4
LOG2_E = math.log2(math.e)
LRU_TIME_TILE = 256
HG_CHUNK = 128
HG_LEVELS = int(math.log2(HG_CHUNK))
HG_SEQS = 4
FFN_CHUNK = 1408

IN_MLA = (0, 512)
IN_LRU = (512, 1024)
IN_SB = (1024, 1792)
IN_HG = (1792, 2816)
D_IN_PAD = 2816


def _params(*semantics):
    return pltpu.CompilerParams(dimension_semantics=semantics, vmem_limit_bytes=VMEM_LIMIT_BYTES)


def _rms(x, g):
    return x * lax.rsqrt(jnp.mean(x * x, axis=-1, keepdims=True) + NORM_EPS) * g


def _dot(a, b):
    return jnp.dot(a, b, preferred_element_type=F32)


def _dot_nt(a, b):
    return lax.dot_general(a, b, (((1,), (1,)), ((), ())), preferred_element_type=F32)


def _dot_tn(a, b):
    return lax.dot_general(a, b, (((0,), (0,)), ((), ())), preferred_element_type=F32)


def _split_bf16(x):
    hi = x.astype(BF16)
    lo = (x - hi.astype(F32)).astype(BF16)
    return hi, lo


def _sigmoid(x):
    return 1.0 / (1.0 + jnp.exp(-x))


def _log_sigmoid(x):
    return jnp.minimum(x, 0.0) - jnp.log1p(jnp.exp(-jnp.abs(x)))


def _rope_table_kernel(pos_ref, freq_ref, m1_ref, m2_ref, c_ref, s1_ref, s2_ref):
    ang = pos_ref[...] * freq_ref[...]
    sn = jnp.sin(ang)
    c_ref[...] = jnp.cos(ang)
    s1_ref[...] = sn * m1_ref[...]
    s2_ref[...] = sn * m2_ref[...]


def _rope_tables(positions):
    m = positions.size
    half = MLA_ROPE // 2
    inv_freq = ROPE_THETA ** (-jnp.arange(half, dtype=F32) / half)
    zeros = jnp.zeros((MLA_NOPE,), F32)
    tail = jnp.zeros((LANES - MLA_NOPE - MLA_ROPE,), F32)
    freq = jnp.concatenate([zeros, inv_freq, inv_freq, tail]).reshape(1, LANES)
    ones = jnp.ones((half,), F32)
    m1 = jnp.concatenate([zeros, -ones, 0 * ones, tail]).reshape(1, LANES)
    m2 = jnp.concatenate([zeros, 0 * ones, ones, tail]).reshape(1, LANES)
    pos = positions.astype(F32).reshape(m, 1)
    tm = ROW_TILE
    row = pl.BlockSpec((1, LANES), lambda i: (0, 0))
    tab = pl.BlockSpec((tm, LANES), lambda i: (i, 0))
    shape = jax.ShapeDtypeStruct((m, LANES), F32)
    return pl.pallas_call(
        _rope_table_kernel,
        out_shape=(shape, shape, shape),
        grid=(m // tm,),
        in_specs=[pl.BlockSpec((tm, 1), lambda i: (i, 0)), row, row, row],
        out_specs=(tab, tab, tab),
        compiler_params=_params("arbitrary"),
        name="rope_tables",
    )(pos, freq, m1, m2)


def _in_proj_kernel(h_ref, g1_ref, win_ref, qg_ref, wuq_ref, kvg_ref, wukv_ref,
                    c_ref, s1_ref, s2_ref,
                    qm_ref, km_ref, vmt_ref, lru_ref, sb_ref, sbvt_ref, hg_ref):
    t = ATTN_TILE
    n_blocks = h_ref.shape[0] // t

    def store_transposed(ref, v):
        for blk in range(n_blocks):
            ref[blk] = v[blk * t:(blk + 1) * t, :].T.astype(BF16)

    u = _rms(h_ref[...], g1_ref[...]).astype(BF16)
    lru_ref[...] = _dot(u, win_ref[:, IN_LRU[0]:IN_LRU[1]])
    sb = _dot(u, win_ref[:, IN_SB[0]:IN_SB[1]])
    sb_ref[...] = sb[:, :2 * GROUP_W].astype(BF16)
    store_transposed(sbvt_ref, sb[:, 2 * GROUP_W:])
    hg_ref[...] = _dot(u, win_ref[:, IN_HG[0]:IN_HG[1]])

    p = _dot(u, win_ref[:, IN_MLA[0]:IN_MLA[1]])
    cq = _rms(p[:, 0:MLA_Q_LORA], qg_ref[...]).astype(BF16)
    ckv = _rms(p[:, MLA_Q_LORA:MLA_Q_LORA + MLA_KV_LORA], kvg_ref[...]).astype(BF16)
    q = _dot(cq, wuq_ref[...])
    kv = _dot(ckv, wukv_ref[...])
    c, s1, s2 = c_ref[...], s1_ref[...], s2_ref[...]

    def rope(x):
        return x * c + pltpu.roll(x, LANES - MLA_ROPE // 2, 1) * s1 + pltpu.roll(x, MLA_ROPE // 2, 1) * s2

    kpe = rope(p[:, 384:512])
    for hh in range(N_HEADS):
        sl = slice(hh * LANES, (hh + 1) * LANES)
        qm_ref[:, sl] = rope(q[:, sl]).astype(BF16)
        km_ref[:, sl] = (kv[:, sl] + kpe).astype(BF16)
    store_transposed(vmt_ref, kv[:, N_HEADS * LANES:])


def _in_proj(h, g1, win, qg, wuq, kvg, wukv, tabs):
    m = h.shape[0]
    tm = ROW_TILE
    t = ATTN_TILE
    rows = lambda w: pl.BlockSpec((tm, w), lambda i: (i, 0))
    full = lambda a: pl.BlockSpec(a.shape, lambda i: (0, 0))
    c, s1, s2 = tabs
    v_t = jax.ShapeDtypeStruct((m // t, GROUP_W, t), BF16)
    out_shape = (
        jax.ShapeDtypeStruct((m, N_HEADS * LANES), BF16),
        jax.ShapeDtypeStruct((m, N_HEADS * LANES), BF16),
        v_t,
        jax.ShapeDtypeStruct((m, 2 * LRU_WIDTH), F32),
        jax.ShapeDtypeStruct((m, 2 * GROUP_W), BF16),
        v_t,
        jax.ShapeDtypeStruct((m, 4 * GROUP_W), F32),
    )

    def out_spec(sd):
        if len(sd.shape) == 3:
            return pl.BlockSpec((tm // t, GROUP_W, t), lambda i: (i, 0, 0))
        return rows(sd.shape[1])
    return pl.pallas_call(
        _in_proj_kernel,
        out_shape=out_shape,
        grid=(m // tm,),
        in_specs=[rows(D_MODEL), full(g1), full(win), full(qg), full(wuq), full(kvg), full(wukv),
                  rows(LANES), rows(LANES), rows(LANES)],
        out_specs=tuple(out_spec(sd) for sd in out_shape),
        compiler_params=_params("arbitrary"),
        name="in_proj",
    )(h, g1, win, qg, wuq, kvg, wukv, c, s1, s2)


def _pair_order(nq):
    dummy = (nq, 0)
    pad = lambda pairs: pairs + [dummy] * (-len(pairs) % ATTN_BODY_BLOCKS)
    diag = pad([(i, i) for i in range(nq)])
    off = pad([(i, i - d) for d in range(1, nq) for i in range(d, nq)])
    order = np.array([dummy] + diag + off + [dummy] * 2, np.int32)
    return (jnp.asarray(order[:, 0]), jnp.asarray(order[:, 1]),
            len(diag) // ATTN_BODY_BLOCKS, len(off) // ATTN_BODY_BLOCKS)


def _mla_kernel(qtab_ref, ktab_ref, q_ref, k_ref, vt_ref, gn_ref, o_ref, m_ref, acc_ref, s0_ref, s1_ref, p_ref,
                *, diag_trips, off_trips):
    t = ATTN_TILE
    nq = q_ref.shape[0] // t
    m_ref[...] = jnp.full(m_ref.shape, MASK_VALUE, F32)
    acc_ref[...] = jnp.zeros(acc_ref.shape, F32)
    n_chunks = t // SOFTMAX_ROWS
    key_idx = lax.broadcasted_iota(jnp.int32, (SOFTMAX_ROWS, t), 0)
    qry_idx = lax.broadcasted_iota(jnp.int32, (SOFTMAX_ROWS, t), 1)
    ones_rows = jnp.ones((SUM_ROWS, t), BF16)
    s_refs = (s0_ref, s1_ref)

    def scores(n, hh, s_ref):
        q0 = pl.multiple_of(jnp.minimum(qtab_ref[n], nq - 1) * t, t)
        k0 = pl.multiple_of(ktab_ref[n] * t, t)
        sl = slice(hh * LANES, (hh + 1) * LANES)
        s_ref[hh] = _dot_nt(k_ref[pl.ds(k0, t), sl], q_ref[pl.ds(q0, t), sl])

    def softmax_pv(n, hh, s_ref, diagonal):
        def chunk(c):
            s = s_ref[hh, c * SOFTMAX_ROWS:(c + 1) * SOFTMAX_ROWS, :]
            if diagonal:
                s = jnp.where(key_idx + c * SOFTMAX_ROWS <= qry_idx, s, MASK_VALUE)
            return s

        qb = qtab_ref[n]
        m_prev = m_ref[qb, hh]
        m_blk = chunk(0)
        for c in range(1, n_chunks):
            m_blk = jnp.maximum(m_blk, chunk(c))
        m_new = jnp.maximum(m_prev, jnp.max(m_blk, axis=0, keepdims=True))
        alpha = jnp.exp2(m_prev - m_new)
        for c in range(n_chunks):
            p_ref[hh, c * SOFTMAX_ROWS:(c + 1) * SOFTMAX_ROWS, :] = jnp.exp2(chunk(c) - m_new).astype(BF16)
        v_ext = jnp.concatenate([vt_ref[ktab_ref[n], hh * MLA_V:(hh + 1) * MLA_V, :], ones_rows], axis=0)
        acc_ref[qb, hh] = alpha * acc_ref[qb, hh] + _dot(v_ext, p_ref[hh])
        m_ref[qb, hh] = m_new

    def sweep(first, trips, diagonal):
        def body(r, carry):
            start = first + ATTN_BODY_BLOCKS * r
            for i in range(ATTN_BODY_BLOCKS):
                for hh in range(N_HEADS):
                    scores(start + i + 1, hh, s_refs[(i + 1) % 2])
                    softmax_pv(start + i, hh, s_refs[i % 2], diagonal)
            return carry

        lax.fori_loop(0, trips, body, 0)

    for hh in range(N_HEADS):
        scores(1, hh, s0_ref)
    sweep(1, diag_trips, True)
    sweep(1 + ATTN_BODY_BLOCKS * diag_trips, off_trips, False)

    def write_out(i, carry):
        r0 = pl.multiple_of(i * t, t)
        y_t = jnp.concatenate([acc_ref[i, hh, 0:MLA_V, :] / acc_ref[i, hh, MLA_V:MLA_V + 1, :]
                               for hh in range(N_HEADS)], axis=0)
        o_ref[pl.ds(r0, t), :] = _rms(y_t.T, gn_ref[...]).astype(BF16)
        return carry

    lax.fori_loop(0, nq, write_out, 0)


def _mla_attention(q, k, v_t, gn, batch):
    m = q.shape[0]
    s = m // batch
    t = ATTN_TILE
    nq = s // t
    qtab, ktab, diag_trips, off_trips = _pair_order(nq)
    smem = pl.BlockSpec(memory_space=pltpu.SMEM)
    return pl.pallas_call(
        functools.partial(_mla_kernel, diag_trips=diag_trips, off_trips=off_trips),
        out_shape=jax.ShapeDtypeStruct((m, GROUP_W), BF16),
        grid=(batch,),
        in_specs=[smem, smem,
                  pl.BlockSpec((s, N_HEADS * LANES), lambda b: (b, 0)),
                  pl.BlockSpec((s, N_HEADS * LANES), lambda b: (b, 0)),
                  pl.BlockSpec((nq, GROUP_W, t), lambda b: (b, 0, 0)),
                  pl.BlockSpec((1, GROUP_W), lambda b: (0, 0))],
        out_specs=pl.BlockSpec((s, GROUP_W), lambda b: (b, 0)),
        scratch_shapes=[pltpu.VMEM((nq + 1, N_HEADS, 1, t), F32),
                        pltpu.VMEM((nq + 1, N_HEADS, MLA_V + SUM_ROWS, t), F32),
                        pltpu.VMEM((N_HEADS, t, t), F32),
                        pltpu.VMEM((N_HEADS, t, t), F32),
                        pltpu.VMEM((N_HEADS, t, t), BF16)],
        compiler_params=_params("arbitrary"),
        name="mla_attention",
    )(qtab, ktab, q, k, v_t, gn)


def _sb_kernel(qtab_ref, ktab_ref, q_ref, k_ref, vt_ref, tri_ref, gn_ref, o_ref,
               qm_ref, carry_ref, acc_ref, z0_ref, z1_ref, lb_ref, lk_ref, later_ref, w_ref,
               *, diag_trips, off_trips):
    t = ATTN_TILE
    nq = q_ref.shape[0] // t
    n_chunks = t // SOFTMAX_ROWS
    lane = lax.broadcasted_iota(jnp.int32, (t, LANES), 1)

    def mask_queries(i, carry):
        r0 = pl.multiple_of(i * t, t)
        for hh in range(N_HEADS):
            qp = q_ref[pl.ds(r0, t), (hh // 2) * LANES:(hh // 2 + 1) * LANES]
            own = (lane >= (hh % 2) * HEAD_DIM) & (lane < (hh % 2 + 1) * HEAD_DIM)
            qm_ref[hh, pl.ds(r0, t), :] = jnp.where(own, qp, jnp.zeros_like(qp))
        return carry

    lax.fori_loop(0, nq, mask_queries, 0)
    qm_ref[:, nq * t:(nq + 1) * t, :] = jnp.zeros((N_HEADS, t, LANES), BF16)
    carry_ref[...] = jnp.zeros(carry_ref.shape, F32)
    acc_ref[...] = jnp.zeros(acc_ref.shape, F32)
    lb_ref[...] = jnp.zeros(lb_ref.shape, F32)
    later_ref[...] = jnp.zeros(later_ref.shape, F32)
    key_idx = lax.broadcasted_iota(jnp.int32, (SOFTMAX_ROWS, t), 0)
    qry_idx = lax.broadcasted_iota(jnp.int32, (SOFTMAX_ROWS, t), 1)

    def rows(c):
        return slice(c * SOFTMAX_ROWS, (c + 1) * SOFTMAX_ROWS)

    def logits(n, hh, z_ref):
        q0 = pl.multiple_of(qtab_ref[n] * t, t)
        k0 = pl.multiple_of(ktab_ref[n] * t, t)
        ps = slice((hh // 2) * LANES, (hh // 2 + 1) * LANES)
        z_ref[hh] = _dot_nt(k_ref[pl.ds(k0, t), ps], qm_ref[hh, pl.ds(q0, t), :])

    def log_terms(hh, z_ref, diagonal):
        for c in range(n_chunks):
            z = z_ref[hh, rows(c), :]
            softplus = jnp.log2(1.0 + jnp.exp2(-jnp.abs(z)))
            log_beta = jnp.minimum(z, 0.0) - softplus
            log_keep = log_beta - z
            if diagonal:
                log_keep = jnp.where(key_idx + c * SOFTMAX_ROWS < qry_idx, log_keep, 0.0)
            lb_ref[hh, rows(c), :] = log_beta
            lk_ref[hh, rows(c), :] = log_keep.astype(BF16)
        later_ref[hh] = _dot(tri_ref[...], lk_ref[hh])

    def weights_pv(n, hh, diagonal):
        qb = qtab_ref[n]
        carry = carry_ref[qb, hh]
        for c in range(n_chunks):
            w = jnp.exp2(lb_ref[hh, rows(c), :] + later_ref[hh, rows(c), :] + carry)
            if diagonal:
                w = jnp.where(key_idx + c * SOFTMAX_ROWS < qry_idx, w, 0.0)
            w_ref[hh, rows(c), :] = w.astype(BF16)
        carry_ref[qb, hh] = carry + later_ref[hh, t:t + 1, :]
        v_t = vt_ref[ktab_ref[n], hh * HEAD_DIM:(hh + 1) * HEAD_DIM, :]
        acc_ref[qb, hh] += _dot(v_t, w_ref[hh])

    tail_heads = range(N_HEADS - PIPE_DEPTH, N_HEADS)
    z_refs = (z0_ref, z1_ref)

    def trip(first, resumed, diagonal):
        pending = [(resumed, hh, diagonal) for hh in tail_heads]
        for i in range(ATTN_BODY_BLOCKS):
            for hh in range(N_HEADS):
                ahead = hh + PIPE_DEPTH
                if ahead < N_HEADS:
                    logits(first + i, ahead, z_refs[i % 2])
                else:
                    logits(first + i + 1, ahead - N_HEADS, z_refs[(i + 1) % 2])
                log_terms(hh, z_refs[i % 2], diagonal)
                pending.append((first + i, hh, diagonal))
                weights_pv(*pending.pop(0))

    def sweep(first, trips, diagonal):
        def body(r, carry):
            start = first + ATTN_BODY_BLOCKS * r
            trip(start, jnp.where(r == 0, 0, start - 1), diagonal)
            return carry

        lax.fori_loop(0, trips, body, 0)
        last = first + ATTN_BODY_BLOCKS * trips - 1
        for hh in tail_heads:
            weights_pv(last, hh, diagonal)

    for hh in range(PIPE_DEPTH):
        logits(1, hh, z0_ref)
    sweep(1, diag_trips, True)
    sweep(1 + ATTN_BODY_BLOCKS * diag_trips, off_trips, False)

    def write_out(i, carry):
        r0 = pl.multiple_of(i * t, t)
        y_t = jnp.concatenate([acc_ref[i, hh] for hh in range(N_HEADS)], axis=0)
        o_ref[pl.ds(r0, t), :] = _rms(y_t.T, gn_ref[...]).astype(BF16)
        return carry

    lax.fori_loop(0, nq, write_out, 0)


def _sb_attention(qk, v_t, gn, batch):
    m = qk.shape[0]
    s = m // batch
    t = ATTN_TILE
    nq = s // t
    idx = np.arange(t)
    later = idx[None, :] > idx[:, None]
    total = np.ones((SUM_ROWS, t), bool)
    tri = jnp.asarray(np.concatenate([later, total], axis=0), BF16)
    qtab, ktab, diag_trips, off_trips = _pair_order(nq)
    smem = pl.BlockSpec(memory_space=pltpu.SMEM)
    return pl.pallas_call(
        functools.partial(_sb_kernel, diag_trips=diag_trips, off_trips=off_trips),
        out_shape=jax.ShapeDtypeStruct((m, GROUP_W), BF16),
        grid=(batch,),
        in_specs=[smem, smem,
                  pl.BlockSpec((s, GROUP_W), lambda b: (b, 0)),
                  pl.BlockSpec((s, GROUP_W), lambda b: (b, 1)),
                  pl.BlockSpec((nq, GROUP_W, t), lambda b: (b, 0, 0)),
                  pl.BlockSpec(tri.shape, lambda b: (0, 0)),
                  pl.BlockSpec((1, GROUP_W), lambda b: (0, 0))],
        out_specs=pl.BlockSpec((s, GROUP_W), lambda b: (b, 0)),
        scratch_shapes=[pltpu.VMEM((N_HEADS, (nq + 1) * t, LANES), BF16),
                        pltpu.VMEM((nq + 1, N_HEADS, 1, t), F32),
                        pltpu.VMEM((nq + 1, N_HEADS, HEAD_DIM, t), F32),
                        pltpu.VMEM((N_HEADS, t, t), F32),
                        pltpu.VMEM((N_HEADS, t, t), F32),
                        pltpu.VMEM((N_HEADS, t, t), F32),
                        pltpu.VMEM((N_HEADS, t, t), BF16),
                        pltpu.VMEM((N_HEADS, t + SUM_ROWS, t), F32),
                        pltpu.VMEM((N_HEADS, t, t), BF16)],
        compiler_params=_params("arbitrary"),
        name="sb_attention",
    )(qtab, ktab, qk, qk, v_t, tri, gn)


def _lru_kernel(x_ref, gate_ref, cw_ref, cb_ref, wab_ref, ba_ref, bx_ref, lam_ref, gn_ref,
                o_ref, xs_ref, a_ref, u_ref, h_ref):
    ts, nb, w = x_ref.shape
    pad = CONV_WIDTH - 1

    @pl.when(pl.program_id(0) == 0)
    def _():
        xs_ref[0:pad] = jnp.zeros((pad, nb, w), F32)
        h_ref[...] = jnp.zeros(h_ref.shape, F32)

    @pl.when(pl.program_id(0) > 0)
    def _():
        xs_ref[0:pad] = xs_ref[ts:ts + pad]

    xs_ref[pad:pad + ts] = x_ref[...]
    xc = cb_ref[...].reshape(1, 1, w)
    for i in range(CONV_WIDTH):
        xc = xc + cw_ref[i:i + 1, :].reshape(1, 1, w) * xs_ref[i:i + ts]
    flat = xc.reshape(ts * nb, w)
    gates = _dot(flat.astype(BF16), wab_ref[...])
    rec_gate = _sigmoid(gates[:, :w] + ba_ref[...])
    in_gate = _sigmoid(gates[:, w:] + bx_ref[...])
    lam = lam_ref[...]
    softplus_neg_lam = jnp.maximum(-lam, 0.0) + jnp.log1p(jnp.exp(-jnp.abs(lam)))
    log_a = -LRU_C * rec_gate * softplus_neg_lam
    a = jnp.exp(log_a)
    a_ref[...] = a.reshape(ts, nb, w)
    one_minus_a2 = -jnp.tanh(log_a) * (a * a + 1.0)
    u_ref[...] = (jnp.sqrt(jnp.maximum(one_minus_a2, 0.0)) * (in_gate * flat)).reshape(ts, nb, w)

    def step(i, h):
        h = a_ref[i] * h + u_ref[i]
        u_ref[i] = h
        return h

    h_ref[...] = lax.fori_loop(0, ts, step, h_ref[...], unroll=8)

    g = gate_ref[...]
    gelu = 0.5 * g * (1.0 + jnp.tanh(math.sqrt(2.0 / math.pi) * (g + 0.044715 * (g * g * g))))
    o_ref[...] = _rms(u_ref[...] * gelu, gn_ref[...].reshape(1, 1, w))


def _rglru(x_t, gate_t, cw, cb, wab, ba, bx, lam, gn):
    s, nb, w = x_t.shape
    ts = LRU_TIME_TILE
    tile = pl.BlockSpec((ts, nb, w), lambda i: (i, 0, 0))
    full = lambda a: pl.BlockSpec(a.shape, lambda i: (0,) * a.ndim)
    return pl.pallas_call(
        _lru_kernel,
        out_shape=jax.ShapeDtypeStruct((s, nb, w), F32),
        grid=(s // ts,),
        in_specs=[tile, tile, full(cw), full(cb), full(wab), full(ba), full(bx), full(lam), full(gn)],
        out_specs=tile,
        scratch_shapes=[pltpu.VMEM((ts + CONV_WIDTH - 1, nb, w), F32),
                        pltpu.VMEM((ts, nb, w), F32),
                        pltpu.VMEM((ts, nb, w), F32),
                        pltpu.VMEM((nb, w), F32)],
        compiler_params=_params("arbitrary"),
        name="rglru",
    )(x_t, gate_t, cw, cb, wab, ba, bx, lam, gn)


def _hg_constants():
    t = HG_CHUNK
    idx = np.arange(t)
    sums = [idx[:, None] >= idx[None, :]]
    masks = [idx[:, None] == idx[None, :]]
    for lev in range(HG_LEVELS):
        half = 1 << lev
        mid = (idx // (2 * half)) * (2 * half) + half
        upper = idx >= mid
        j = idx[None, :]
        up_rows = upper[:, None] & (j >= mid[:, None]) & (j <= idx[:, None])
        low_rows = (~upper)[:, None] & (j > idx[:, None]) & (j < mid[:, None])
        sums.append(up_rows | low_rows)
        same = (idx[:, None] // (2 * half)) == (idx[None, :] // (2 * half))
        masks.append(same & upper[:, None] & (~upper)[None, :])
    sum_mat = np.concatenate(sums, axis=0)
    sum_mat = jnp.asarray(np.concatenate([sum_mat, sum_mat], axis=1), BF16)
    mask = jnp.asarray(np.stack(masks, axis=0), F32)
    head = np.arange(GROUP_W) // HEAD_DIM
    same_head = head[:, None] == head[None, :]
    same_head = jnp.asarray(np.concatenate([same_head, same_head], axis=0), BF16)
    return sum_mat, mask, same_head


def _hg_kernel(x_ref, lbl_ref, ng_ref, sum_ref, mask_ref, same_ref, o_ref, st_ref, *, layer):
    t = HG_CHUNK
    w = GROUP_W

    seqs = range(x_ref.shape[0])

    @pl.when(pl.program_id(1) == 0)
    def _():
        st_ref[...] = jnp.zeros(st_ref.shape, F32)

    logits = lbl_ref[...]
    ex = jnp.exp(logits - jnp.max(logits, axis=0, keepdims=True))
    prob = ex / jnp.sum(ex, axis=0, keepdims=True)
    csum = prob[0:1, :]
    for i in range(1, layer + 1):
        csum = csum + prob[i:i + 1, :]
    lb = csum - prob[0:1, :]
    a_ = jnp.log(jnp.maximum(lb, LB_FLOOR))
    same = same_ref[...]
    lane_head = lax.broadcasted_iota(jnp.int32, (t, w), 1) // HEAD_DIM

    def by_head(x):
        return jnp.concatenate([jnp.where(lane_head == hh, x, jnp.zeros_like(x)) for hh in range(N_HEADS)], axis=0)

    q = [x_ref[b, :, 0:w] for b in seqs]
    vb = [x_ref[b, :, 2 * w:3 * w].astype(BF16) for b in seqs]
    key, sums = [], []
    for b in seqs:
        fz = x_ref[b, :, w:2 * w]
        b_ = jnp.log1p(-lb) + _log_sigmoid(fz)
        log_f = jnp.maximum(a_, b_) + jnp.log1p(jnp.exp(-jnp.abs(a_ - b_)))
        key.append((1.0 - lb) * _sigmoid(-fz))
        sums.append(_dot(sum_ref[...], jnp.concatenate(_split_bf16(log_f), axis=0)))

    out = []
    for b in seqs:
        cum = sums[b][0:t]
        last = cum[t - 1:t, :]
        st = st_ref[b]
        out.append(_dot_nt((q[b] * jnp.exp(cum)).astype(BF16), st.astype(BF16)))
        kdec = (key[b] * jnp.exp(last - cum)).astype(BF16)
        st_ref[b] = st * jnp.exp(last) + jnp.where(same[0:w] > 0, _dot_tn(vb[b], kdec), 0.0)

    scores = [[jnp.zeros((t, t), F32) for _ in range(N_HEADS)] for _ in seqs]
    for lev in range(HG_LEVELS + 1):
        for b in seqs:
            if lev == 0:
                qt, kt = q[b], key[b]
            else:
                decay = jnp.exp(sums[b][lev * t:(lev + 1) * t])
                qt, kt = q[b] * decay, key[b] * decay
            sc = _dot_nt(qt.astype(BF16), by_head(kt.astype(BF16)))
            for hh in range(N_HEADS):
                scores[b][hh] = scores[b][hh] + sc[:, hh * t:(hh + 1) * t] * mask_ref[lev]
    for b in seqs:
        all_scores = jnp.concatenate([s.astype(BF16) for s in scores[b]], axis=1)
        out[b] = out[b] + _dot(all_scores, by_head(vb[b]))

    mean_sq = [_dot(jnp.concatenate(_split_bf16(out[b] * out[b]), axis=1), same) * (1.0 / HEAD_DIM) for b in seqs]
    for b in seqs:
        g = x_ref[b, :, 3 * w:4 * w]
        normed = out[b] * lax.rsqrt(mean_sq[b] + NORM_EPS) * ng_ref[...]
        o_ref[b] = (normed * (g * _sigmoid(g))).astype(BF16)


def _hgrn2(x, lb_logits, ng, layer, batch):
    m = x.shape[0]
    t = HG_CHUNK
    nb = HG_SEQS
    seq = m // batch
    sum_mat, mask, same_head = _hg_constants()
    full = lambda a: pl.BlockSpec(a.shape, lambda b, c: (0,) * a.ndim)
    y = pl.pallas_call(
        functools.partial(_hg_kernel, layer=layer),
        out_shape=jax.ShapeDtypeStruct((batch, seq, GROUP_W), BF16),
        grid=(batch // nb, seq // t),
        in_specs=[pl.BlockSpec((nb, t, 4 * GROUP_W), lambda b, c: (b, c, 0)),
                  full(lb_logits), full(ng), full(sum_mat), full(mask), full(same_head)],
        out_specs=pl.BlockSpec((nb, t, GROUP_W), lambda b, c: (b, c, 0)),
        scratch_shapes=[pltpu.VMEM((nb, GROUP_W, GROUP_W), F32)],
        compiler_params=_params("arbitrary", "arbitrary"),
        name="hgrn2",
    )(x.reshape(batch, seq, 4 * GROUP_W), lb_logits, ng, sum_mat, mask, same_head)
    return y.reshape(m, GROUP_W)


def _out_ffn_kernel(h_ref, ya_ref, yb_ref, yc_ref, yd_ref, wout_ref, g2_ref, wg_ref, wu_ref, wd_ref,
                    gf_ref, o_ref, *, final):
    mixed = jnp.concatenate([ya_ref[...], yb_ref[...], yc_ref[...], yd_ref[...]], axis=1)
    h1 = h_ref[...] + _dot(mixed, wout_ref[...])
    un = _rms(h1, g2_ref[...]).astype(BF16)
    acc = h1
    for c in range(FFN_HIDDEN // FFN_CHUNK):
        sl = slice(c * FFN_CHUNK, (c + 1) * FFN_CHUNK)
        gt = _dot(un, wg_ref[:, sl])
        up = _dot(un, wu_ref[:, sl])
        act = (gt * _sigmoid(gt) * up).astype(BF16)
        acc = acc + _dot(act, wd_ref[sl, :])
    if final:
        acc = _rms(acc, gf_ref[...])
    o_ref[...] = acc


def _out_ffn(h, ya, yb, yc, yd, wout, g2, wg, wu, wd, gf, final):
    m = h.shape[0]
    tm = ROW_TILE
    rows = lambda w: pl.BlockSpec((tm, w), lambda i: (i, 0))
    once = lambda a: pl.BlockSpec(a.shape, lambda i: (0, 0), pipeline_mode=pl.Buffered(1))
    return pl.pallas_call(
        functools.partial(_out_ffn_kernel, final=final),
        out_shape=jax.ShapeDtypeStruct((m, D_MODEL), F32),
        grid=(m // tm,),
        in_specs=[rows(D_MODEL), rows(GROUP_W), rows(GROUP_W), rows(GROUP_W), rows(GROUP_W),
                  once(wout), once(g2), once(wg), once(wu), once(wd), once(gf)],
        out_specs=rows(D_MODEL),
        compiler_params=_params("arbitrary"),
        name="out_ffn",
    )(h, ya, yb, yc, yd, wout, g2, wg, wu, wd, gf)


def _pad_in_proj(w):
    d = w.shape[0]
    z = lambda n: jnp.zeros((d, n), w.dtype)
    sb_q = w[:, 928:928 + GROUP_W] * (LOG2_E * HEAD_DIM ** -0.5)
    return jnp.concatenate([w[:, :384], z(MLA_NOPE), w[:, 384:416], z(LANES - MLA_NOPE - MLA_ROPE),
                            w[:, 416:928], sb_q, w[:, 928 + GROUP_W:]], axis=1).astype(BF16)


def _pad_heads(w, width):
    k = w.shape[0]
    w = w.reshape(k, N_HEADS, width)
    w = jnp.pad(w, ((0, 0), (0, 0), (0, LANES - width)))
    return w.reshape(k, N_HEADS * LANES)


def _block_diag(w):
    n, c, d = w.shape
    eye = jnp.eye(n, dtype=w.dtype)
    return (eye[:, None, :, None] * w[:, :, None, :]).reshape(n * c, n * d)


def kernel(x, positions, ln1_g, w_in, mla_q_norm_g, mla_w_uq, mla_kv_norm_g, mla_w_ukv, lru_conv_w, lru_conv_b, lru_w_a, lru_b_a, lru_w_x, lru_b_x, lru_lambda, hgrn_lb_logits, hgrn_norm_g, group_norm_g, w_out, ln2_g, w_ffn_gate, w_ffn_up, w_ffn_down, final_norm_g):
    batch, seq, d_model = x.shape
    depth = w_in.shape[0]
    m = batch * seq
    row = lambda v: v.reshape(1, -1)
    h = x.reshape(m, d_model)
    tabs = _rope_tables(positions)
    qk_scale = LOG2_E * (MLA_NOPE + MLA_ROPE) ** -0.5
    for l in range(depth):
        wuq = _pad_heads(mla_w_uq[l] * qk_scale, MLA_NOPE + MLA_ROPE).astype(BF16)
        wukv = mla_w_ukv[l].reshape(MLA_KV_LORA, N_HEADS, MLA_NOPE + MLA_V)
        wuk = _pad_heads(wukv[:, :, :MLA_NOPE].reshape(MLA_KV_LORA, -1), MLA_NOPE)
        wuv = wukv[:, :, MLA_NOPE:].reshape(MLA_KV_LORA, -1)
        wukv_p = jnp.concatenate([wuk, wuv], axis=1).astype(BF16)
        qm, km, vm_t, lru_in, sb_qk, sb_vt, hg_in = _in_proj(
            h, row(ln1_g[l]), _pad_in_proj(w_in[l]), row(mla_q_norm_g[l]), wuq,
            row(mla_kv_norm_g[l]), wukv_p, tabs)

        y_a = _mla_attention(qm, km, vm_t, row(group_norm_g[l, 0]), batch)

        to_time_major = lambda a: a.reshape(batch, seq, LRU_WIDTH).transpose(1, 0, 2)
        wab = jnp.concatenate([_block_diag(lru_w_a[l]), _block_diag(lru_w_x[l])], axis=1).astype(BF16)
        y_b = _rglru(to_time_major(lru_in[:, :LRU_WIDTH]), to_time_major(lru_in[:, LRU_WIDTH:]),
                     lru_conv_w[l], row(lru_conv_b[l]), wab, row(lru_b_a[l]), row(lru_b_x[l]),
                     row(lru_lambda[l]), row(group_norm_g[l, 1]))
        y_b = y_b.transpose(1, 0, 2).reshape(m, LRU_WIDTH).astype(BF16)

        y_c = _sb_attention(sb_qk, sb_vt, row(group_norm_g[l, 2]), batch)
        y_d = _hgrn2(hg_in, hgrn_lb_logits, row(hgrn_norm_g[l]), l, batch)

        h = _out_ffn(h, y_a, y_b, y_c, y_d, w_out[l].astype(BF16), row(ln2_g[l]),
                     w_ffn_gate[l].astype(BF16), w_ffn_up[l].astype(BF16), w_ffn_down[l].astype(BF16),
                     row(final_norm_g), final=(l == depth - 1))
    return h.reshape(batch, seq, d_model)
```

```python
import functools
import math

import numpy as np
import jax
import jax.numpy as jnp
from jax import lax
from jax.experimental import pallas as pl
from jax.experimental.pallas import tpu as pltpu

F32 = jnp.float32
BF16 = jnp.bfloat16

D_MODEL = 1024
N_HEADS = 4
MLA_Q_LORA = 256
MLA_KV_LORA = 128
MLA_NOPE = 64
MLA_ROPE = 32
MLA_V = 64
ROPE_THETA = 10000.0
LRU_WIDTH = 256
CONV_WIDTH = 4
LRU_C = 8.0
HEAD_DIM = 64
GROUP_W = 256
FFN_HIDDEN = 2816
NORM_EPS = 1e-6
MASK_VALUE = -1e30
LB_FLOOR = 1e-30

LANES = 128
SUBLANES = 8
VMEM_LIMIT_BYTES = 56 * 1024 * 1024

ROW_TILE = 512
ATTN_TILE = 256
SOFTMAX_ROWS = 64
SUM_ROWS = 16
PIPE_DEPTH = 2
ATTN_BODY_BLOCKS = 8
LOG2_E = math.log2(math.e)
LRU_TIME_TILE = 256
HG_CHUNK = 128
HG_LEVELS = int(math.log2(HG_CHUNK))
HG_SEQS = 4
FFN_CHUNK = 1408

IN_MLA = (0, 512)
IN_LRU = (512, 1024)
IN_SB = (1024, 1792)
IN_HG = (1792, 2816)
D_IN_PAD = 2816


def _params(*semantics):
    return pltpu.CompilerParams(dimension_semantics=semantics, vmem_limit_bytes=VMEM_LIMIT_BYTES)


def _rms(x, g):
    return x * lax.rsqrt(jnp.mean(x * x, axis=-1, keepdims=True) + NORM_EPS) * g


def _dot(a, b):
    return jnp.dot(a, b, preferred_element_type=F32)


def _dot_nt(a, b):
    return lax.dot_general(a, b, (((1,), (1,)), ((), ())), preferred_element_type=F32)


def _dot_tn(a, b):
    return lax.dot_general(a, b, (((0,), (0,)), ((), ())), preferred_element_type=F32)


def _split_bf16(x):
    hi = x.astype(BF16)
    lo = (x - hi.astype(F32)).astype(BF16)
    return hi, lo


def _sigmoid(x):
    return 1.0 / (1.0 + jnp.exp(-x))


def _log_sigmoid(x):
    return jnp.minimum(x, 0.0) - jnp.log1p(jnp.exp(-jnp.abs(x)))


def _rope_table_kernel(pos_ref, freq_ref, m1_ref, m2_ref, c_ref, s1_ref, s2_ref):
    ang = pos_ref[...] * freq_ref[...]
    sn = jnp.sin(ang)
    c_ref[...] = jnp.cos(ang)
    s1_ref[...] = sn * m1_ref[...]
    s2_ref[...] = sn * m2_ref[...]


def _rope_tables(positions):
    m = positions.size
    half = MLA_ROPE // 2
    inv_freq = ROPE_THETA ** (-jnp.arange(half, dtype=F32) / half)
    zeros = jnp.zeros((MLA_NOPE,), F32)
    tail = jnp.zeros((LANES - MLA_NOPE - MLA_ROPE,), F32)
    freq = jnp.concatenate([zeros, inv_freq, inv_freq, tail]).reshape(1, LANES)
    ones = jnp.ones((half,), F32)
    m1 = jnp.concatenate([zeros, -ones, 0 * ones, tail]).reshape(1, LANES)
    m2 = jnp.concatenate([zeros, 0 * ones, ones, tail]).reshape(1, LANES)
    pos = positions.astype(F32).reshape(m, 1)
    tm = ROW_TILE
    row = pl.BlockSpec((1, LANES), lambda i: (0, 0))
    tab = pl.BlockSpec((tm, LANES), lambda i: (i, 0))
    shape = jax.ShapeDtypeStruct((m, LANES), F32)
    return pl.pallas_call(
        _rope_table_kernel,
        out_shape=(shape, shape, shape),
        grid=(m // tm,),
        in_specs=[pl.BlockSpec((tm, 1), lambda i: (i, 0)), row, row, row],
        out_specs=(tab, tab, tab),
        compiler_params=_params("arbitrary"),
        name="rope_tables",
    )(pos, freq, m1, m2)


def _in_proj_kernel(h_ref, g1_ref, win_ref, qg_ref, wuq_ref, kvg_ref, wukv_ref,
                    c_ref, s1_ref, s2_ref,
                    qm_ref, km_ref, vmt_ref, lrux_ref, lrug_ref, sb_ref, sbvt_ref, hg_ref):
    t = ATTN_TILE
    n_blocks = h_ref.shape[0] // t

    def store_transposed(ref, v):
        for blk in range(n_blocks):
            ref[blk] = v[blk * t:(blk + 1) * t, :].T.astype(BF16)

    u = _rms(h_ref[...], g1_ref[...]).astype(BF16)
    lrux_ref[...] = _dot(u, win_ref[:, IN_LRU[0]:IN_LRU[0] + LRU_WIDTH])
    lrug_ref[...] = _dot(u, win_ref[:, IN_LRU[0] + LRU_WIDTH:IN_LRU[1]])
    sb = _dot(u, win_ref[:, IN_SB[0]:IN_SB[1]])
    sb_ref[...] = sb[:, :2 * GROUP_W].astype(BF16)
    store_transposed(sbvt_ref, sb[:, 2 * GROUP_W:])
    hg_ref[...] = _dot(u, win_ref[:, IN_HG[0]:IN_HG[1]])

    p = _dot(u, win_ref[:, IN_MLA[0]:IN_MLA[1]])
    cq = _rms(p[:, 0:MLA_Q_LORA], qg_ref[...]).astype(BF16)
    ckv = _rms(p[:, MLA_Q_LORA:MLA_Q_LORA + MLA_KV_LORA], kvg_ref[...]).astype(BF16)
    q = _dot(cq, wuq_ref[...])
    kv = _dot(ckv, wukv_ref[...])
    c, s1, s2 = c_ref[...], s1_ref[...], s2_ref[...]

    def rope(x):
        return x * c + pltpu.roll(x, LANES - MLA_ROPE // 2, 1) * s1 + pltpu.roll(x, MLA_ROPE // 2, 1) * s2

    kpe = rope(p[:, 384:512])
    for hh in range(N_HEADS):
        sl = slice(hh * LANES, (hh + 1) * LANES)
        qm_ref[:, sl] = rope(q[:, sl]).astype(BF16)
        km_ref[:, sl] = (kv[:, sl] + kpe).astype(BF16)
    store_transposed(vmt_ref, kv[:, N_HEADS * LANES:])


def _in_proj(h, g1, win, qg, wuq, kvg, wukv, tabs, layer, batch):
    m = h.shape[0]
    seq = m // batch
    tm = ROW_TILE
    t = ATTN_TILE
    rows = lambda w: pl.BlockSpec((tm, w), lambda i: (i, 0))

    def full(a):
        if a.ndim == 3:
            return pl.BlockSpec((None,) + a.shape[1:], lambda i: (layer, 0, 0))
        return pl.BlockSpec(a.shape, lambda i: (0, 0))
    c, s1, s2 = tabs
    v_t = jax.ShapeDtypeStruct((m // t, GROUP_W, t), BF16)
    out_shape = (
        jax.ShapeDtypeStruct((m, N_HEADS * LANES), BF16),
        jax.ShapeDtypeStruct((m, N_HEADS * LANES), BF16),
        v_t,
        jax.ShapeDtypeStruct((seq, batch * LRU_WIDTH), F32),
        jax.ShapeDtypeStruct((seq, batch * LRU_WIDTH), F32),
        jax.ShapeDtypeStruct((m, 2 * GROUP_W), BF16),
        v_t,
        jax.ShapeDtypeStruct((m, 4 * GROUP_W), F32),
    )
    tiles_per_seq = seq // tm

    def out_spec(sd):
        if sd.shape[0] == seq:
            return pl.BlockSpec((tm, LRU_WIDTH), lambda i: (i % tiles_per_seq, i // tiles_per_seq))
        if len(sd.shape) == 3:
            return pl.BlockSpec((tm // t, GROUP_W, t), lambda i: (i, 0, 0))
        return rows(sd.shape[1])
    return pl.pallas_call(
        _in_proj_kernel,
        out_shape=out_shape,
        grid=(m // tm,),
        in_specs=[rows(D_MODEL), full(g1), full(win), full(qg), full(wuq), full(kvg), full(wukv),
                  rows(LANES), rows(LANES), rows(LANES)],
        out_specs=tuple(out_spec(sd) for sd in out_shape),
        compiler_params=_params("arbitrary"),
        name="in_proj",
    )(h, g1, win, qg, wuq, kvg, wukv, c, s1, s2)


def _pair_order(nq):
    dummy = (nq, 0)
    pad = lambda pairs: pairs + [dummy] * (-len(pairs) % ATTN_BODY_BLOCKS)
    diag = pad([(i, i) for i in range(nq)])
    off = pad([(i, i - d) for d in range(1, nq) for i in range(d, nq)])
    order = np.array([dummy] + diag + off + [dummy] * 2, np.int32)
    return (jnp.asarray(order[:, 0]), jnp.asarray(order[:, 1]),
            len(diag) // ATTN_BODY_BLOCKS, len(off) // ATTN_BODY_BLOCKS)


def _mla_kernel(qtab_ref, ktab_ref, q_ref, k_ref, vt_ref, gn_ref, o_ref, m_ref, acc_ref, s0_ref, s1_ref, p_ref,
                *, diag_trips, off_trips):
    t = ATTN_TILE
    nq = q_ref.shape[0] // t
    m_ref[...] = jnp.full(m_ref.shape, MASK_VALUE, F32)
    acc_ref[...] = jnp.zeros(acc_ref.shape, F32)
    n_chunks = t // SOFTMAX_ROWS
    key_idx = lax.broadcasted_iota(jnp.int32, (SOFTMAX_ROWS, t), 0)
    qry_idx = lax.broadcasted_iota(jnp.int32, (SOFTMAX_ROWS, t), 1)
    ones_rows = jnp.ones((SUM_ROWS, t), BF16)
    s_refs = (s0_ref, s1_ref)

    def scores(n, hh, s_ref):
        q0 = pl.multiple_of(jnp.minimum(qtab_ref[n], nq - 1) * t, t)
        k0 = pl.multiple_of(ktab_ref[n] * t, t)
        sl = slice(hh * LANES, (hh + 1) * LANES)
        s_ref[hh] = _dot_nt(k_ref[pl.ds(k0, t), sl], q_ref[pl.ds(q0, t), sl])

    def softmax_pv(n, hh, s_ref, diagonal):
        def chunk(c):
            s = s_ref[hh, c * SOFTMAX_ROWS:(c + 1) * SOFTMAX_ROWS, :]
            if diagonal:
                s = jnp.where(key_idx + c * SOFTMAX_ROWS <= qry_idx, s, MASK_VALUE)
            return s

        qb = qtab_ref[n]
        m_prev = m_ref[qb, hh]
        m_blk = chunk(0)
        for c in range(1, n_chunks):
            m_blk = jnp.maximum(m_blk, chunk(c))
        m_new = jnp.maximum(m_prev, jnp.max(m_blk, axis=0, keepdims=True))
        alpha = jnp.exp2(m_prev - m_new)
        for c in range(n_chunks):
            p_ref[hh, c * SOFTMAX_ROWS:(c + 1) * SOFTMAX_ROWS, :] = jnp.exp2(chunk(c) - m_new).astype(BF16)
        v_ext = jnp.concatenate([vt_ref[ktab_ref[n], hh * MLA_V:(hh + 1) * MLA_V, :], ones_rows], axis=0)
        acc_ref[qb, hh] = alpha * acc_ref[qb, hh] + _dot(v_ext, p_ref[hh])
        m_ref[qb, hh] = m_new

    def sweep(first, trips, diagonal):
        def body(r, carry):
            start = first + ATTN_BODY_BLOCKS * r
            for i in range(ATTN_BODY_BLOCKS):
                for hh in range(N_HEADS):
                    scores(start + i + 1, hh, s_refs[(i + 1) % 2])
                    softmax_pv(start + i, hh, s_refs[i % 2], diagonal)
            return carry

        lax.fori_loop(0, trips, body, 0)

    for hh in range(N_HEADS):
        scores(1, hh, s0_ref)
    sweep(1, diag_trips, True)
    sweep(1 + ATTN_BODY_BLOCKS * diag_trips, off_trips, False)

    def write_out(i, carry):
        r0 = pl.multiple_of(i * t, t)
        y_t = jnp.concatenate([acc_ref[i, hh, 0:MLA_V, :] / acc_ref[i, hh, MLA_V:MLA_V + 1, :]
                               for hh in range(N_HEADS)], axis=0)
        o_ref[pl.ds(r0, t), :] = _rms(y_t.T, gn_ref[...]).astype(BF16)
        return carry

    lax.fori_loop(0, nq, write_out, 0)


def _mla_attention(q, k, v_t, gn, batch):
    m = q.shape[0]
    s = m // batch
    t = ATTN_TILE
    nq = s // t
    qtab, ktab, diag_trips, off_trips = _pair_order(nq)
    smem = pl.BlockSpec(memory_space=pltpu.SMEM)
    return pl.pallas_call(
        functools.partial(_mla_kernel, diag_trips=diag_trips, off_trips=off_trips),
        out_shape=jax.ShapeDtypeStruct((m, GROUP_W), BF16),
        grid=(batch,),
        in_specs=[smem, smem,
                  pl.BlockSpec((s, N_HEADS * LANES), lambda b: (b, 0)),
                  pl.BlockSpec((s, N_HEADS * LANES), lambda b: (b, 0)),
                  pl.BlockSpec((nq, GROUP_W, t), lambda b: (b, 0, 0)),
                  pl.BlockSpec((1, GROUP_W), lambda b: (0, 0))],
        out_specs=pl.BlockSpec((s, GROUP_W), lambda b: (b, 0)),
        scratch_shapes=[pltpu.VMEM((nq + 1, N_HEADS, 1, t), F32),
                        pltpu.VMEM((nq + 1, N_HEADS, MLA_V + SUM_ROWS, t), F32),
                        pltpu.VMEM((N_HEADS, t, t), F32),
                        pltpu.VMEM((N_HEADS, t, t), F32),
                        pltpu.VMEM((N_HEADS, t, t), BF16)],
        compiler_params=_params("arbitrary"),
        name="mla_attention",
    )(qtab, ktab, q, k, v_t, gn)


def _sb_kernel(qtab_ref, ktab_ref, q_ref, k_ref, vt_ref, tri_ref, gn_ref, o_ref,
               qm_ref, carry_ref, acc_ref, z0_ref, z1_ref, lb_ref, lk_ref, later_ref, w_ref,
               *, diag_trips, off_trips):
    t = ATTN_TILE
    nq = q_ref.shape[0] // t
    n_chunks = t // SOFTMAX_ROWS
    lane = lax.broadcasted_iota(jnp.int32, (t, LANES), 1)

    def mask_queries(i, carry):
        r0 = pl.multiple_of(i * t, t)
        for hh in range(N_HEADS):
            qp = q_ref[pl.ds(r0, t), (hh // 2) * LANES:(hh // 2 + 1) * LANES]
            own = (lane >= (hh % 2) * HEAD_DIM) & (lane < (hh % 2 + 1) * HEAD_DIM)
            qm_ref[hh, pl.ds(r0, t), :] = jnp.where(own, qp, jnp.zeros_like(qp))
        return carry

    lax.fori_loop(0, nq, mask_queries, 0)
    qm_ref[:, nq * t:(nq + 1) * t, :] = jnp.zeros((N_HEADS, t, LANES), BF16)
    carry_ref[...] = jnp.zeros(carry_ref.shape, F32)
    acc_ref[...] = jnp.zeros(acc_ref.shape, F32)
    lb_ref[...] = jnp.zeros(lb_ref.shape, F32)
    later_ref[...] = jnp.zeros(later_ref.shape, F32)
    key_idx = lax.broadcasted_iota(jnp.int32, (SOFTMAX_ROWS, t), 0)
    qry_idx = lax.broadcasted_iota(jnp.int32, (SOFTMAX_ROWS, t), 1)

    def rows(c):
        return slice(c * SOFTMAX_ROWS, (c + 1) * SOFTMAX_ROWS)

    def logits(n, hh, z_ref):
        q0 = pl.multiple_of(qtab_ref[n] * t, t)
        k0 = pl.multiple_of(ktab_ref[n] * t, t)
        ps = slice((hh // 2) * LANES, (hh // 2 + 1) * LANES)
        z_ref[hh] = _dot_nt(k_ref[pl.ds(k0, t), ps], qm_ref[hh, pl.ds(q0, t), :])

    def log_terms(hh, z_ref, diagonal):
        for c in range(n_chunks):
            z = z_ref[hh, rows(c), :]
            softplus = jnp.log2(1.0 + jnp.exp2(-jnp.abs(z)))
            log_beta = jnp.minimum(z, 0.0) - softplus
            log_keep = log_beta - z
            if diagonal:
                log_keep = jnp.where(key_idx + c * SOFTMAX_ROWS < qry_idx, log_keep, 0.0)
            lb_ref[hh, rows(c), :] = log_beta
            lk_ref[hh, rows(c), :] = log_keep.astype(BF16)
        later_ref[hh] = _dot(tri_ref[...], lk_ref[hh])

    def weights_pv(n, hh, diagonal):
        qb = qtab_ref[n]
        carry = carry_ref[qb, hh]
        for c in range(n_chunks):
            w = jnp.exp2(lb_ref[hh, rows(c), :] + later_ref[hh, rows(c), :] + carry)
            if diagonal:
                w = jnp.where(key_idx + c * SOFTMAX_ROWS < qry_idx, w, 0.0)
            w_ref[hh, rows(c), :] = w.astype(BF16)
        carry_ref[qb, hh] = carry + later_ref[hh, t:t + 1, :]
        v_t = vt_ref[ktab_ref[n], hh * HEAD_DIM:(hh + 1) * HEAD_DIM, :]
        acc_ref[qb, hh] += _dot(v_t, w_ref[hh])

    tail_heads = range(N_HEADS - PIPE_DEPTH, N_HEADS)
    z_refs = (z0_ref, z1_ref)

    def trip(first, resumed, diagonal):
        pending = [(resumed, hh, diagonal) for hh in tail_heads]
        for i in range(ATTN_BODY_BLOCKS):
            for hh in range(N_HEADS):
                ahead = hh + PIPE_DEPTH
                if ahead < N_HEADS:
                    logits(first + i, ahead, z_refs[i % 2])
                else:
                    logits(first + i + 1, ahead - N_HEADS, z_refs[(i + 1) % 2])
                log_terms(hh, z_refs[i % 2], diagonal)
                pending.append((first + i, hh, diagonal))
                weights_pv(*pending.pop(0))

    def sweep(first, trips, diagonal):
        def body(r, carry):
            start = first + ATTN_BODY_BLOCKS * r
            trip(start, jnp.where(r == 0, 0, start - 1), diagonal)
            return carry

        lax.fori_loop(0, trips, body, 0)
        last = first + ATTN_BODY_BLOCKS * trips - 1
        for hh in tail_heads:
            weights_pv(last, hh, diagonal)

    for hh in range(PIPE_DEPTH):
        logits(1, hh, z0_ref)
    sweep(1, diag_trips, True)
    sweep(1 + ATTN_BODY_BLOCKS * diag_trips, off_trips, False)

    def write_out(i, carry):
        r0 = pl.multiple_of(i * t, t)
        y_t = jnp.concatenate([acc_ref[i, hh] for hh in range(N_HEADS)], axis=0)
        o_ref[pl.ds(r0, t), :] = _rms(y_t.T, gn_ref[...]).astype(BF16)
        return carry

    lax.fori_loop(0, nq, write_out, 0)


def _sb_attention(qk, v_t, gn, batch):
    m = qk.shape[0]
    s = m // batch
    t = ATTN_TILE
    nq = s // t
    idx = np.arange(t)
    later = idx[None, :] > idx[:, None]
    total = np.ones((SUM_ROWS, t), bool)
    tri = jnp.asarray(np.concatenate([later, total], axis=0), BF16)
    qtab, ktab, diag_trips, off_trips = _pair_order(nq)
    smem = pl.BlockSpec(memory_space=pltpu.SMEM)
    return pl.pallas_call(
        functools.partial(_sb_kernel, diag_trips=diag_trips, off_trips=off_trips),
        out_shape=jax.ShapeDtypeStruct((m, GROUP_W), BF16),
        grid=(batch,),
        in_specs=[smem, smem,
                  pl.BlockSpec((s, GROUP_W), lambda b: (b, 0)),
                  pl.BlockSpec((s, GROUP_W), lambda b: (b, 1)),
                  pl.BlockSpec((nq, GROUP_W, t), lambda b: (b, 0, 0)),
                  pl.BlockSpec(tri.shape, lambda b: (0, 0)),
                  pl.BlockSpec((1, GROUP_W), lambda b: (0, 0))],
        out_specs=pl.BlockSpec((s, GROUP_W), lambda b: (b, 0)),
        scratch_shapes=[pltpu.VMEM((N_HEADS, (nq + 1) * t, LANES), BF16),
                        pltpu.VMEM((nq + 1, N_HEADS, 1, t), F32),
                        pltpu.VMEM((nq + 1, N_HEADS, HEAD_DIM, t), F32),
                        pltpu.VMEM((N_HEADS, t, t), F32),
                        pltpu.VMEM((N_HEADS, t, t), F32),
                        pltpu.VMEM((N_HEADS, t, t), F32),
                        pltpu.VMEM((N_HEADS, t, t), BF16),
                        pltpu.VMEM((N_HEADS, t + SUM_ROWS, t), F32),
                        pltpu.VMEM((N_HEADS, t, t), BF16)],
        compiler_params=_params("arbitrary"),
        name="sb_attention",
    )(qtab, ktab, qk, qk, v_t, tri, gn)


def _lru_kernel(x_ref, gate_ref, cw_ref, cb_ref, wab_ref, ba_ref, bx_ref, lam_ref, gn_ref,
                o_ref, xs_ref, a_ref, u_ref, h_ref):
    ts, wide = x_ref.shape
    w = LRU_WIDTH
    seqs = range(wide // w)
    pad = CONV_WIDTH - 1
    hist = SUBLANES

    @pl.when(pl.program_id(0) == 0)
    def _():
        xs_ref[0:hist] = jnp.zeros((hist, wide), F32)
        h_ref[...] = jnp.zeros(h_ref.shape, F32)

    @pl.when(pl.program_id(0) > 0)
    def _():
        xs_ref[0:hist] = xs_ref[ts:ts + hist]

    xs_ref[hist:hist + ts] = x_ref[...]
    lam = lam_ref[...]
    softplus_neg_lam = jnp.maximum(-lam, 0.0) + jnp.log1p(jnp.exp(-jnp.abs(lam)))
    for b in seqs:
        sl = slice(b * w, (b + 1) * w)
        xc = cb_ref[...]
        for i in range(CONV_WIDTH):
            xc = xc + cw_ref[i:i + 1, :] * xs_ref[hist - pad + i:hist - pad + i + ts, sl]
        gates = _dot(xc.astype(BF16), wab_ref[...])
        rec_gate = _sigmoid(gates[:, :w] + ba_ref[...])
        in_gate = _sigmoid(gates[:, w:] + bx_ref[...])
        log_a = -LRU_C * rec_gate * softplus_neg_lam
        a = jnp.exp(log_a)
        a_ref[:, sl] = a
        one_minus_a2 = -jnp.tanh(log_a) * (a * a + 1.0)
        u_ref[:, sl] = jnp.sqrt(jnp.maximum(one_minus_a2, 0.0)) * (in_gate * xc)

    def step(i, h):
        h = a_ref[pl.ds(i, 1), :] * h + u_ref[pl.ds(i, 1), :]
        u_ref[pl.ds(i, 1), :] = h
        return h

    h_ref[...] = lax.fori_loop(0, ts, step, h_ref[...], unroll=8)

    for b in seqs:
        sl = slice(b * w, (b + 1) * w)
        g = gate_ref[:, sl]
        gelu = 0.5 * g * (1.0 + jnp.tanh(math.sqrt(2.0 / math.pi) * (g + 0.044715 * (g * g * g))))
        o_ref[:, sl] = _rms(u_ref[:, sl] * gelu, gn_ref[...])


def _rglru(x_t, gate_t, cw, cb, wab, ba, bx, lam, gn):
    s, wide = x_t.shape
    ts = LRU_TIME_TILE
    tile = pl.BlockSpec((ts, wide), lambda i: (i, 0))
    full = lambda a: pl.BlockSpec(a.shape, lambda i: (0,) * a.ndim)
    return pl.pallas_call(
        _lru_kernel,
        out_shape=jax.ShapeDtypeStruct((s, wide), F32),
        grid=(s // ts,),
        in_specs=[tile, tile, full(cw), full(cb), full(wab), full(ba), full(bx), full(lam), full(gn)],
        out_specs=tile,
        scratch_shapes=[pltpu.VMEM((ts + SUBLANES, wide), F32),
                        pltpu.VMEM((ts, wide), F32),
                        pltpu.VMEM((ts, wide), F32),
                        pltpu.VMEM((1, wide), F32)],
        compiler_params=_params("arbitrary"),
        name="rglru",
    )(x_t, gate_t, cw, cb, wab, ba, bx, lam, gn)


def _hg_constants():
    t = HG_CHUNK
    idx = np.arange(t)
    sums = [idx[:, None] >= idx[None, :]]
    masks = [idx[:, None] == idx[None, :]]
    for lev in range(HG_LEVELS):
        half = 1 << lev
        mid = (idx // (2 * half)) * (2 * half) + half
        upper = idx >= mid
        j = idx[None, :]
        up_rows = upper[:, None] & (j >= mid[:, None]) & (j <= idx[:, None])
        low_rows = (~upper)[:, None] & (j > idx[:, None]) & (j < mid[:, None])
        sums.append(up_rows | low_rows)
        same = (idx[:, None] // (2 * half)) == (idx[None, :] // (2 * half))
        masks.append(same & upper[:, None] & (~upper)[None, :])
    sum_mat = np.concatenate(sums, axis=0)
    sum_mat = jnp.asarray(np.concatenate([sum_mat, sum_mat], axis=1), BF16)
    mask = jnp.asarray(np.stack(masks, axis=0), F32)
    head = np.arange(GROUP_W) // HEAD_DIM
    same_head = head[:, None] == head[None, :]
    same_head = jnp.asarray(np.concatenate([same_head, same_head], axis=0), BF16)
    return sum_mat, mask, same_head


def _hg_kernel(x_ref, lbl_ref, ng_ref, sum_ref, mask_ref, same_ref, o_ref, st_ref, *, layer):
    t = HG_CHUNK
    w = GROUP_W

    seqs = range(x_ref.shape[0])

    @pl.when(pl.program_id(1) == 0)
    def _():
        st_ref[...] = jnp.zeros(st_ref.shape, F32)

    logits = lbl_ref[...]
    ex = jnp.exp(logits - jnp.max(logits, axis=0, keepdims=True))
    prob = ex / jnp.sum(ex, axis=0, keepdims=True)
    csum = prob[0:1, :]
    for i in range(1, layer + 1):
        csum = csum + prob[i:i + 1, :]
    lb = csum - prob[0:1, :]
    a_ = jnp.log(jnp.maximum(lb, LB_FLOOR))
    same = same_ref[...]
    lane_head = lax.broadcasted_iota(jnp.int32, (t, w), 1) // HEAD_DIM

    def by_head(x):
        return jnp.concatenate([jnp.where(lane_head == hh, x, jnp.zeros_like(x)) for hh in range(N_HEADS)], axis=0)

    q = [x_ref[b, :, 0:w] for b in seqs]
    vb = [x_ref[b, :, 2 * w:3 * w].astype(BF16) for b in seqs]
    key, sums = [], []
    for b in seqs:
        fz = x_ref[b, :, w:2 * w]
        b_ = jnp.log1p(-lb) + _log_sigmoid(fz)
        log_f = jnp.maximum(a_, b_) + jnp.log1p(jnp.exp(-jnp.abs(a_ - b_)))
        key.append((1.0 - lb) * _sigmoid(-fz))
        sums.append(_dot(sum_ref[...], jnp.concatenate(_split_bf16(log_f), axis=0)))

    out = []
    for b in seqs:
        cum = sums[b][0:t]
        last = cum[t - 1:t, :]
        st = st_ref[b]
        out.append(_dot_nt((q[b] * jnp.exp(cum)).astype(BF16), st.astype(BF16)))
        kdec = (key[b] * jnp.exp(last - cum)).astype(BF16)
        st_ref[b] = st * jnp.exp(last) + jnp.where(same[0:w] > 0, _dot_tn(vb[b], kdec), 0.0)

    scores = [[jnp.zeros((t, t), F32) for _ in range(N_HEADS)] for _ in seqs]
    for lev in range(HG_LEVELS + 1):
        for b in seqs:
            if lev == 0:
                qt, kt = q[b], key[b]
            else:
                decay = jnp.exp(sums[b][lev * t:(lev + 1) * t])
                qt, kt = q[b] * decay, key[b] * decay
            sc = _dot_nt(qt.astype(BF16), by_head(kt.astype(BF16)))
            for hh in range(N_HEADS):
                scores[b][hh] = scores[b][hh] + sc[:, hh * t:(hh + 1) * t] * mask_ref[lev]
    for b in seqs:
        all_scores = jnp.concatenate([s.astype(BF16) for s in scores[b]], axis=1)
        out[b] = out[b] + _dot(all_scores, by_head(vb[b]))

    mean_sq = [_dot(jnp.concatenate(_split_bf16(out[b] * out[b]), axis=1), same) * (1.0 / HEAD_DIM) for b in seqs]
    for b in seqs:
        g = x_ref[b, :, 3 * w:4 * w]
        normed = out[b] * lax.rsqrt(mean_sq[b] + NORM_EPS) * ng_ref[...]
        o_ref[b] = (normed * (g * _sigmoid(g))).astype(BF16)


def _hgrn2(x, lb_logits, ng, layer, batch):
    m = x.shape[0]
    t = HG_CHUNK
    nb = HG_SEQS
    seq = m // batch
    sum_mat, mask, same_head = _hg_constants()
    full = lambda a: pl.BlockSpec(a.shape, lambda b, c: (0,) * a.ndim)
    y = pl.pallas_call(
        functools.partial(_hg_kernel, layer=layer),
        out_shape=jax.ShapeDtypeStruct((batch, seq, GROUP_W), BF16),
        grid=(batch // nb, seq // t),
        in_specs=[pl.BlockSpec((nb, t, 4 * GROUP_W), lambda b, c: (b, c, 0)),
                  full(lb_logits), full(ng), full(sum_mat), full(mask), full(same_head)],
        out_specs=pl.BlockSpec((nb, t, GROUP_W), lambda b, c: (b, c, 0)),
        scratch_shapes=[pltpu.VMEM((nb, GROUP_W, GROUP_W), F32)],
        compiler_params=_params("arbitrary", "arbitrary"),
        name="hgrn2",
    )(x.reshape(batch, seq, 4 * GROUP_W), lb_logits, ng, sum_mat, mask, same_head)
    return y.reshape(m, GROUP_W)


def _out_ffn_kernel(h_ref, ya_ref, yb_ref, yc_ref, yd_ref, wout_ref, g2_ref, wg_ref, wu_ref, wd_ref,
                    gf_ref, o_ref, *, final):
    mixed = jnp.concatenate([ya_ref[...], yb_ref[...].astype(BF16), yc_ref[...], yd_ref[...]], axis=1)
    h1 = h_ref[...] + _dot(mixed, wout_ref[...])
    un = _rms(h1, g2_ref[...]).astype(BF16)
    acc = h1
    for c in range(FFN_HIDDEN // FFN_CHUNK):
        sl = slice(c * FFN_CHUNK, (c + 1) * FFN_CHUNK)
        gt = _dot(un, wg_ref[:, sl])
        up = _dot(un, wu_ref[:, sl])
        act = (gt * _sigmoid(gt) * up).astype(BF16)
        acc = acc + _dot(act, wd_ref[sl, :])
    if final:
        acc = _rms(acc, gf_ref[...])
    o_ref[...] = acc


def _out_ffn(h, ya, yb, yc, yd, wout, g2, wg, wu, wd, gf, layer, final):
    m = h.shape[0]
    tm = ROW_TILE
    rows = lambda w: pl.BlockSpec((tm, w), lambda i: (i, 0))

    def once(a):
        if a.ndim == 3:
            return pl.BlockSpec((None,) + a.shape[1:], lambda i: (layer, 0, 0), pipeline_mode=pl.Buffered(1))
        return pl.BlockSpec(a.shape, lambda i: (0, 0), pipeline_mode=pl.Buffered(1))
    tiles_per_seq = yb.shape[0] // tm
    yb_rows = pl.BlockSpec((tm, GROUP_W), lambda i: (i % tiles_per_seq, i // tiles_per_seq))
    return pl.pallas_call(
        functools.partial(_out_ffn_kernel, final=final),
        out_shape=jax.ShapeDtypeStruct((m, D_MODEL), F32),
        grid=(m // tm,),
        in_specs=[rows(D_MODEL), rows(GROUP_W), yb_rows, rows(GROUP_W), rows(GROUP_W),
                  once(wout), once(g2), once(wg), once(wu), once(wd), once(gf)],
        out_specs=rows(D_MODEL),
        compiler_params=_params("arbitrary"),
        name="out_ffn",
    )(h, ya, yb, yc, yd, wout, g2, wg, wu, wd, gf)


def _pad_in_proj(w):
    z = lambda n: jnp.zeros(w.shape[:-1] + (n,), w.dtype)
    sb_q = w[..., 928:928 + GROUP_W] * (LOG2_E * HEAD_DIM ** -0.5)
    return jnp.concatenate([w[..., :384], z(MLA_NOPE), w[..., 384:416], z(LANES - MLA_NOPE - MLA_ROPE),
                            w[..., 416:928], sb_q, w[..., 928 + GROUP_W:]], axis=-1).astype(BF16)


def _pad_heads(w, width):
    lead = w.shape[:-1]
    w = w.reshape(lead + (N_HEADS, width))
    w = jnp.pad(w, ((0, 0),) * (len(lead) + 1) + ((0, LANES - width),))
    return w.reshape(lead + (N_HEADS * LANES,))


def _block_diag(w):
    n, c, d = w.shape[-3:]
    eye = jnp.eye(n, dtype=w.dtype)
    return (eye[:, None, :, None] * w[..., :, :, None, :]).reshape(w.shape[:-3] + (n * c, n * d))


def kernel(x, positions, ln1_g, w_in, mla_q_norm_g, mla_w_uq, mla_kv_norm_g, mla_w_ukv, lru_conv_w, lru_conv_b, lru_w_a, lru_b_a, lru_w_x, lru_b_x, lru_lambda, hgrn_lb_logits, hgrn_norm_g, group_norm_g, w_out, ln2_g, w_ffn_gate, w_ffn_up, w_ffn_down, final_norm_g):
    batch, seq, d_model = x.shape
    depth = w_in.shape[0]
    m = batch * seq
    row = lambda v: v.reshape(1, -1)
    h = x.reshape(m, d_model)
    tabs = _rope_tables(positions)
    qk_scale = LOG2_E * (MLA_NOPE + MLA_ROPE) ** -0.5

    win = _pad_in_proj(w_in)
    wuq = _pad_heads(mla_w_uq * qk_scale, MLA_NOPE + MLA_ROPE).astype(BF16)
    wukv = mla_w_ukv.reshape(depth, MLA_KV_LORA, N_HEADS, MLA_NOPE + MLA_V)
    wuk = _pad_heads(wukv[..., :MLA_NOPE].reshape(depth, MLA_KV_LORA, -1), MLA_NOPE)
    wuv = wukv[..., MLA_NOPE:].reshape(depth, MLA_KV_LORA, -1)
    wukv = jnp.concatenate([wuk, wuv], axis=-1).astype(BF16)
    wab = jnp.concatenate([_block_diag(lru_w_a), _block_diag(lru_w_x)], axis=-1).astype(BF16)
    wout, wg, wu, wd = (w.astype(BF16) for w in (w_out, w_ffn_gate, w_ffn_up, w_ffn_down))

    for l in range(depth):
        qm, km, vm_t, lru_x, lru_gate, sb_qk, sb_vt, hg_in = _in_proj(
            h, row(ln1_g[l]), win, row(mla_q_norm_g[l]), wuq, row(mla_kv_norm_g[l]), wukv, tabs, l, batch)

        y_a = _mla_attention(qm, km, vm_t, row(group_norm_g[l, 0]), batch)
        y_b = _rglru(lru_x, lru_gate, lru_conv_w[l], row(lru_conv_b[l]), wab[l], row(lru_b_a[l]),
                     row(lru_b_x[l]), row(lru_lambda[l]), row(group_norm_g[l, 1]))

        y_c = _sb_attention(sb_qk, sb_vt, row(group_norm_g[l, 2]), batch)
        y_d = _hgrn2(hg_in, hgrn_lb_logits, row(hgrn_norm_g[l]), l, batch)

        h = _out_ffn(h, y_a, y_b, y_c, y_d, wout, row(ln2_g[l]), wg, wu, wd, row(final_norm_g),
                     l, final=(l == depth - 1))
    return h.reshape(batch, seq, d_model)
```

```python
import functools
import math

import numpy as np
import jax
import jax.numpy as jnp
from jax import lax
from jax.experimental import pallas as pl
from jax.experimental.pallas import tpu as pltpu

F32 = jnp.float32
BF16 = jnp.bfloat16

D_MODEL = 1024
N_HEADS = 4
MLA_Q_LORA = 256
MLA_KV_LORA = 128
MLA_NOPE = 64
MLA_ROPE = 32
MLA_V = 64
ROPE_THETA = 10000.0
LRU_WIDTH = 256
CONV_WIDTH = 4
LRU_C = 8.0
HEAD_DIM = 64
GROUP_W = 256
FFN_HIDDEN = 2816
NORM_EPS = 1e-6
MASK_VALUE = -1e30
LB_FLOOR = 1e-30

LANES = 128
SUBLANES = 8
VMEM_LIMIT_BYTES = 56 * 1024 * 1024

ROW_TILE = 512
ATTN_TILE = 256
SOFTMAX_ROWS = 64
SUM_ROWS = 16
PIPE_DEPTH = 2
ATTN_BODY_BLOCKS = 8
LOG2_E = math.log2(math.e)
LRU_TIME_TILE = 256
HG_CHUNK = 128
HG_LEVELS = int(math.log2(HG_CHUNK))
HG_SEQS = 4
FFN_CHUNK = 1408

IN_MLA = (0, 512)
IN_LRU = (512, 1024)
IN_SB = (1024, 1792)
IN_HG = (1792, 2816)
D_IN_PAD = 2816


def _params(*semantics):
    return pltpu.CompilerParams(dimension_semantics=semantics, vmem_limit_bytes=VMEM_LIMIT_BYTES)


def _rms(x, g):
    return x * lax.rsqrt(jnp.mean(x * x, axis=-1, keepdims=True) + NORM_EPS) * g


def _dot(a, b):
    return jnp.dot(a, b, preferred_element_type=F32)


def _dot_nt(a, b):
    return lax.dot_general(a, b, (((1,), (1,)), ((), ())), preferred_element_type=F32)


def _dot_tn(a, b):
    return lax.dot_general(a, b, (((0,), (0,)), ((), ())), preferred_element_type=F32)


def _split_bf16(x):
    hi = x.astype(BF16)
    lo = (x - hi.astype(F32)).astype(BF16)
    return hi, lo


def _sigmoid(x):
    return 1.0 / (1.0 + jnp.exp(-x))


def _log_sigmoid(x):
    return jnp.minimum(x, 0.0) - jnp.log1p(jnp.exp(-jnp.abs(x)))


def _rope_table_kernel(pos_ref, freq_ref, m1_ref, m2_ref, c_ref, s1_ref, s2_ref):
    ang = pos_ref[...] * freq_ref[...]
    sn = jnp.sin(ang)
    c_ref[...] = jnp.cos(ang)
    s1_ref[...] = sn * m1_ref[...]
    s2_ref[...] = sn * m2_ref[...]


def _rope_tables(positions):
    m = positions.size
    half = MLA_ROPE // 2
    inv_freq = ROPE_THETA ** (-jnp.arange(half, dtype=F32) / half)
    zeros = jnp.zeros((MLA_NOPE,), F32)
    tail = jnp.zeros((LANES - MLA_NOPE - MLA_ROPE,), F32)
    freq = jnp.concatenate([zeros, inv_freq, inv_freq, tail]).reshape(1, LANES)
    ones = jnp.ones((half,), F32)
    m1 = jnp.concatenate([zeros, -ones, 0 * ones, tail]).reshape(1, LANES)
    m2 = jnp.concatenate([zeros, 0 * ones, ones, tail]).reshape(1, LANES)
    pos = positions.astype(F32).reshape(m, 1)
    tm = ROW_TILE
    row = pl.BlockSpec((1, LANES), lambda i: (0, 0))
    tab = pl.BlockSpec((tm, LANES), lambda i: (i, 0))
    shape = jax.ShapeDtypeStruct((m, LANES), F32)
    return pl.pallas_call(
        _rope_table_kernel,
        out_shape=(shape, shape, shape),
        grid=(m // tm,),
        in_specs=[pl.BlockSpec((tm, 1), lambda i: (i, 0)), row, row, row],
        out_specs=(tab, tab, tab),
        compiler_params=_params("arbitrary"),
        name="rope_tables",
    )(pos, freq, m1, m2)


def _in_proj_kernel(h_ref, g1_ref, win_ref, qg_ref, wuq_ref, kvg_ref, wukv_ref,
                    c_ref, s1_ref, s2_ref,
                    qm_ref, km_ref, vmt_ref, lrux_ref, lrug_ref, sb_ref, sbvt_ref, hg_ref):
    t = ATTN_TILE
    n_blocks = h_ref.shape[0] // t

    def store_transposed(ref, v):
        for blk in range(n_blocks):
            ref[blk] = v[blk * t:(blk + 1) * t, :].T.astype(BF16)

    u = _rms(h_ref[...], g1_ref[...]).astype(BF16)
    p = _dot(u, win_ref[:, IN_MLA[0]:IN_MLA[1]])
    lrux_ref[...] = _dot(u, win_ref[:, IN_LRU[0]:IN_LRU[0] + LRU_WIDTH])
    lrug_ref[...] = _dot(u, win_ref[:, IN_LRU[0] + LRU_WIDTH:IN_LRU[1]])
    cq = _rms(p[:, 0:MLA_Q_LORA], qg_ref[...]).astype(BF16)
    ckv = _rms(p[:, MLA_Q_LORA:MLA_Q_LORA + MLA_KV_LORA], kvg_ref[...]).astype(BF16)
    q = _dot(cq, wuq_ref[...])
    kv = _dot(ckv, wukv_ref[...])
    sb = _dot(u, win_ref[:, IN_SB[0]:IN_SB[1]])
    c, s1, s2 = c_ref[...], s1_ref[...], s2_ref[...]

    def rope(x):
        return x * c + pltpu.roll(x, LANES - MLA_ROPE // 2, 1) * s1 + pltpu.roll(x, MLA_ROPE // 2, 1) * s2

    kpe = rope(p[:, 384:512])
    for hh in range(N_HEADS):
        sl = slice(hh * LANES, (hh + 1) * LANES)
        qm_ref[:, sl] = rope(q[:, sl]).astype(BF16)
        km_ref[:, sl] = (kv[:, sl] + kpe).astype(BF16)
    store_transposed(vmt_ref, kv[:, N_HEADS * LANES:])
    hg_ref[...] = _dot(u, win_ref[:, IN_HG[0]:IN_HG[1]])
    sb_ref[...] = sb[:, :2 * GROUP_W].astype(BF16)
    store_transposed(sbvt_ref, sb[:, 2 * GROUP_W:])


def _in_proj(h, g1, win, qg, wuq, kvg, wukv, tabs, layer, batch):
    m = h.shape[0]
    seq = m // batch
    tm = ROW_TILE
    t = ATTN_TILE
    rows = lambda w: pl.BlockSpec((tm, w), lambda i: (i, 0))

    def full(a):
        if a.ndim == 3:
            return pl.BlockSpec((None,) + a.shape[1:], lambda i: (layer, 0, 0))
        return pl.BlockSpec(a.shape, lambda i: (0, 0))
    c, s1, s2 = tabs
    v_t = jax.ShapeDtypeStruct((m // t, GROUP_W, t), BF16)
    out_shape = (
        jax.ShapeDtypeStruct((m, N_HEADS * LANES), BF16),
        jax.ShapeDtypeStruct((m, N_HEADS * LANES), BF16),
        v_t,
        jax.ShapeDtypeStruct((seq, batch * LRU_WIDTH), F32),
        jax.ShapeDtypeStruct((seq, batch * LRU_WIDTH), F32),
        jax.ShapeDtypeStruct((m, 2 * GROUP_W), BF16),
        v_t,
        jax.ShapeDtypeStruct((m, 4 * GROUP_W), F32),
    )
    tiles_per_seq = seq // tm

    def out_spec(sd):
        if sd.shape[0] == seq:
            return pl.BlockSpec((tm, LRU_WIDTH), lambda i: (i % tiles_per_seq, i // tiles_per_seq))
        if len(sd.shape) == 3:
            return pl.BlockSpec((tm // t, GROUP_W, t), lambda i: (i, 0, 0))
        return rows(sd.shape[1])
    return pl.pallas_call(
        _in_proj_kernel,
        out_shape=out_shape,
        grid=(m // tm,),
        in_specs=[rows(D_MODEL), full(g1), full(win), full(qg), full(wuq), full(kvg), full(wukv),
                  rows(LANES), rows(LANES), rows(LANES)],
        out_specs=tuple(out_spec(sd) for sd in out_shape),
        compiler_params=_params("arbitrary"),
        name="in_proj",
    )(h, g1, win, qg, wuq, kvg, wukv, c, s1, s2)


def _pair_order(nq):
    dummy = (nq, 0)
    pad = lambda pairs: pairs + [dummy] * (-len(pairs) % ATTN_BODY_BLOCKS)
    diag = pad([(i, i) for i in range(nq)])
    off = pad([(i, i - d) for d in range(1, nq) for i in range(d, nq)])
    order = np.array([dummy] + diag + off + [dummy] * 2, np.int32)
    return (jnp.asarray(order[:, 0]), jnp.asarray(order[:, 1]),
            len(diag) // ATTN_BODY_BLOCKS, len(off) // ATTN_BODY_BLOCKS)


def _mla_kernel(qtab_ref, ktab_ref, q_ref, k_ref, vt_ref, gn_ref, o_ref, m_ref, acc_ref, s0_ref, s1_ref, p_ref,
                *, diag_trips, off_trips):
    t = ATTN_TILE
    nq = q_ref.shape[0] // t
    m_ref[...] = jnp.full(m_ref.shape, MASK_VALUE, F32)
    acc_ref[...] = jnp.zeros(acc_ref.shape, F32)
    n_chunks = t // SOFTMAX_ROWS
    key_idx = lax.broadcasted_iota(jnp.int32, (SOFTMAX_ROWS, t), 0)
    qry_idx = lax.broadcasted_iota(jnp.int32, (SOFTMAX_ROWS, t), 1)
    ones_rows = jnp.ones((SUM_ROWS, t), BF16)
    s_refs = (s0_ref, s1_ref)

    def scores(n, hh, s_ref):
        q0 = pl.multiple_of(jnp.minimum(qtab_ref[n], nq - 1) * t, t)
        k0 = pl.multiple_of(ktab_ref[n] * t, t)
        sl = slice(hh * LANES, (hh + 1) * LANES)
        s_ref[hh] = _dot_nt(k_ref[pl.ds(k0, t), sl], q_ref[pl.ds(q0, t), sl])

    def softmax_pv(n, hh, s_ref, diagonal):
        def chunk(c):
            s = s_ref[hh, c * SOFTMAX_ROWS:(c + 1) * SOFTMAX_ROWS, :]
            if diagonal:
                s = jnp.where(key_idx + c * SOFTMAX_ROWS <= qry_idx, s, MASK_VALUE)
            return s

        qb = qtab_ref[n]
        m_prev = m_ref[qb, hh]
        m_blk = chunk(0)
        for c in range(1, n_chunks):
            m_blk = jnp.maximum(m_blk, chunk(c))
        m_new = jnp.maximum(m_prev, jnp.max(m_blk, axis=0, keepdims=True))
        alpha = jnp.exp2(m_prev - m_new)
        for c in range(n_chunks):
            p_ref[hh, c * SOFTMAX_ROWS:(c + 1) * SOFTMAX_ROWS, :] = jnp.exp2(chunk(c) - m_new).astype(BF16)
        v_ext = jnp.concatenate([vt_ref[ktab_ref[n], hh * MLA_V:(hh + 1) * MLA_V, :], ones_rows], axis=0)
        acc_ref[qb, hh] = alpha * acc_ref[qb, hh] + _dot(v_ext, p_ref[hh])
        m_ref[qb, hh] = m_new

    def sweep(first, trips, diagonal):
        def body(r, carry):
            start = first + ATTN_BODY_BLOCKS * r
            for i in range(ATTN_BODY_BLOCKS):
                for hh in range(N_HEADS):
                    scores(start + i + 1, hh, s_refs[(i + 1) % 2])
                    softmax_pv(start + i, hh, s_refs[i % 2], diagonal)
            return carry

        lax.fori_loop(0, trips, body, 0)

    for hh in range(N_HEADS):
        scores(1, hh, s0_ref)
    sweep(1, diag_trips, True)
    sweep(1 + ATTN_BODY_BLOCKS * diag_trips, off_trips, False)

    def write_out(i, carry):
        r0 = pl.multiple_of(i * t, t)
        y_t = jnp.concatenate([acc_ref[i, hh, 0:MLA_V, :] / acc_ref[i, hh, MLA_V:MLA_V + 1, :]
                               for hh in range(N_HEADS)], axis=0)
        o_ref[pl.ds(r0, t), :] = _rms(y_t.T, gn_ref[...]).astype(BF16)
        return carry

    lax.fori_loop(0, nq, write_out, 0)


def _mla_attention(q, k, v_t, gn, batch):
    m = q.shape[0]
    s = m // batch
    t = ATTN_TILE
    nq = s // t
    qtab, ktab, diag_trips, off_trips = _pair_order(nq)
    smem = pl.BlockSpec(memory_space=pltpu.SMEM)
    return pl.pallas_call(
        functools.partial(_mla_kernel, diag_trips=diag_trips, off_trips=off_trips),
        out_shape=jax.ShapeDtypeStruct((m, GROUP_W), BF16),
        grid=(batch,),
        in_specs=[smem, smem,
                  pl.BlockSpec((s, N_HEADS * LANES), lambda b: (b, 0)),
                  pl.BlockSpec((s, N_HEADS * LANES), lambda b: (b, 0)),
                  pl.BlockSpec((nq, GROUP_W, t), lambda b: (b, 0, 0)),
                  pl.BlockSpec((1, GROUP_W), lambda b: (0, 0))],
        out_specs=pl.BlockSpec((s, GROUP_W), lambda b: (b, 0)),
        scratch_shapes=[pltpu.VMEM((nq + 1, N_HEADS, 1, t), F32),
                        pltpu.VMEM((nq + 1, N_HEADS, MLA_V + SUM_ROWS, t), F32),
                        pltpu.VMEM((N_HEADS, t, t), F32),
                        pltpu.VMEM((N_HEADS, t, t), F32),
                        pltpu.VMEM((N_HEADS, t, t), BF16)],
        compiler_params=_params("arbitrary"),
        name="mla_attention",
    )(qtab, ktab, q, k, v_t, gn)


def _sb_kernel(qtab_ref, ktab_ref, q_ref, k_ref, vt_ref, tri_ref, gn_ref, o_ref,
               qm_ref, carry_ref, acc_ref, z0_ref, z1_ref, lb_ref, lk_ref, later_ref, w_ref,
               *, diag_trips, off_trips):
    t = ATTN_TILE
    nq = q_ref.shape[0] // t
    n_chunks = t // SOFTMAX_ROWS
    lane = lax.broadcasted_iota(jnp.int32, (t, LANES), 1)

    def mask_queries(i, carry):
        r0 = pl.multiple_of(i * t, t)
        for hh in range(N_HEADS):
            qp = q_ref[pl.ds(r0, t), (hh // 2) * LANES:(hh // 2 + 1) * LANES]
            own = (lane >= (hh % 2) * HEAD_DIM) & (lane < (hh % 2 + 1) * HEAD_DIM)
            qm_ref[hh, pl.ds(r0, t), :] = jnp.where(own, qp, jnp.zeros_like(qp))
        return carry

    lax.fori_loop(0, nq, mask_queries, 0)
    qm_ref[:, nq * t:(nq + 1) * t, :] = jnp.zeros((N_HEADS, t, LANES), BF16)
    carry_ref[...] = jnp.zeros(carry_ref.shape, F32)
    acc_ref[...] = jnp.zeros(acc_ref.shape, F32)
    lb_ref[...] = jnp.zeros(lb_ref.shape, F32)
    later_ref[...] = jnp.zeros(later_ref.shape, F32)
    key_idx = lax.broadcasted_iota(jnp.int32, (SOFTMAX_ROWS, t), 0)
    qry_idx = lax.broadcasted_iota(jnp.int32, (SOFTMAX_ROWS, t), 1)

    def rows(c):
        return slice(c * SOFTMAX_ROWS, (c + 1) * SOFTMAX_ROWS)

    def logits(n, hh, z_ref):
        q0 = pl.multiple_of(qtab_ref[n] * t, t)
        k0 = pl.multiple_of(ktab_ref[n] * t, t)
        ps = slice((hh // 2) * LANES, (hh // 2 + 1) * LANES)
        z_ref[hh] = _dot_nt(k_ref[pl.ds(k0, t), ps], qm_ref[hh, pl.ds(q0, t), :])

    def log_terms(hh, z_ref, diagonal):
        for c in range(n_chunks):
            z = z_ref[hh, rows(c), :]
            softplus = jnp.log2(1.0 + jnp.exp2(-jnp.abs(z)))
            log_beta = jnp.minimum(z, 0.0) - softplus
            log_keep = log_beta - z
            if diagonal:
                log_keep = jnp.where(key_idx + c * SOFTMAX_ROWS < qry_idx, log_keep, 0.0)
            lb_ref[hh, rows(c), :] = log_beta
            lk_ref[hh, rows(c), :] = log_keep.astype(BF16)
        later_ref[hh] = _dot(tri_ref[...], lk_ref[hh])

    def weights_pv(n, hh, diagonal):
        qb = qtab_ref[n]
        carry = carry_ref[qb, hh]
        for c in range(n_chunks):
            w = jnp.exp2(lb_ref[hh, rows(c), :] + later_ref[hh, rows(c), :] + carry)
            if diagonal:
                w = jnp.where(key_idx + c * SOFTMAX_ROWS < qry_idx, w, 0.0)
            w_ref[hh, rows(c), :] = w.astype(BF16)
        carry_ref[qb, hh] = carry + later_ref[hh, t:t + 1, :]
        v_t = vt_ref[ktab_ref[n], hh * HEAD_DIM:(hh + 1) * HEAD_DIM, :]
        acc_ref[qb, hh] += _dot(v_t, w_ref[hh])

    tail_heads = range(N_HEADS - PIPE_DEPTH, N_HEADS)
    z_refs = (z0_ref, z1_ref)

    def trip(first, resumed, diagonal):
        pending = [(resumed, hh, diagonal) for hh in tail_heads]
        for i in range(ATTN_BODY_BLOCKS):
            for hh in range(N_HEADS):
                ahead = hh + PIPE_DEPTH
                if ahead < N_HEADS:
                    logits(first + i, ahead, z_refs[i % 2])
                else:
                    logits(first + i + 1, ahead - N_HEADS, z_refs[(i + 1) % 2])
                log_terms(hh, z_refs[i % 2], diagonal)
                pending.append((first + i, hh, diagonal))
                weights_pv(*pending.pop(0))

    def sweep(first, trips, diagonal):
        def body(r, carry):
            start = first + ATTN_BODY_BLOCKS * r
            trip(start, jnp.where(r == 0, 0, start - 1), diagonal)
            return carry

        lax.fori_loop(0, trips, body, 0)
        last = first + ATTN_BODY_BLOCKS * trips - 1
        for hh in tail_heads:
            weights_pv(last, hh, diagonal)

    for hh in range(PIPE_DEPTH):
        logits(1, hh, z0_ref)
    sweep(1, diag_trips, True)
    sweep(1 + ATTN_BODY_BLOCKS * diag_trips, off_trips, False)

    def write_out(i, carry):
        r0 = pl.multiple_of(i * t, t)
        y_t = jnp.concatenate([acc_ref[i, hh] for hh in range(N_HEADS)], axis=0)
        o_ref[pl.ds(r0, t), :] = _rms(y_t.T, gn_ref[...]).astype(BF16)
        return carry

    lax.fori_loop(0, nq, write_out, 0)


def _sb_attention(qk, v_t, gn, batch):
    m = qk.shape[0]
    s = m // batch
    t = ATTN_TILE
    nq = s // t
    idx = np.arange(t)
    later = idx[None, :] > idx[:, None]
    total = np.ones((SUM_ROWS, t), bool)
    tri = jnp.asarray(np.concatenate([later, total], axis=0), BF16)
    qtab, ktab, diag_trips, off_trips = _pair_order(nq)
    smem = pl.BlockSpec(memory_space=pltpu.SMEM)
    return pl.pallas_call(
        functools.partial(_sb_kernel, diag_trips=diag_trips, off_trips=off_trips),
        out_shape=jax.ShapeDtypeStruct((m, GROUP_W), BF16),
        grid=(batch,),
        in_specs=[smem, smem,
                  pl.BlockSpec((s, GROUP_W), lambda b: (b, 0)),
                  pl.BlockSpec((s, GROUP_W), lambda b: (b, 1)),
                  pl.BlockSpec((nq, GROUP_W, t), lambda b: (b, 0, 0)),
                  pl.BlockSpec(tri.shape, lambda b: (0, 0)),
                  pl.BlockSpec((1, GROUP_W), lambda b: (0, 0))],
        out_specs=pl.BlockSpec((s, GROUP_W), lambda b: (b, 0)),
        scratch_shapes=[pltpu.VMEM((N_HEADS, (nq + 1) * t, LANES), BF16),
                        pltpu.VMEM((nq + 1, N_HEADS, 1, t), F32),
                        pltpu.VMEM((nq + 1, N_HEADS, HEAD_DIM, t), F32),
                        pltpu.VMEM((N_HEADS, t, t), F32),
                        pltpu.VMEM((N_HEADS, t, t), F32),
                        pltpu.VMEM((N_HEADS, t, t), F32),
                        pltpu.VMEM((N_HEADS, t, t), BF16),
                        pltpu.VMEM((N_HEADS, t + SUM_ROWS, t), F32),
                        pltpu.VMEM((N_HEADS, t, t), BF16)],
        compiler_params=_params("arbitrary"),
        name="sb_attention",
    )(qtab, ktab, qk, qk, v_t, tri, gn)


def _lru_kernel(x_ref, gate_ref, cw_ref, cb_ref, wab_ref, ba_ref, bx_ref, lam_ref, gn_ref,
                o_ref, xs_ref, a_ref, u_ref, h_ref):
    ts, wide = x_ref.shape
    w = LRU_WIDTH
    seqs = range(wide // w)
    pad = CONV_WIDTH - 1
    hist = SUBLANES

    @pl.when(pl.program_id(0) == 0)
    def _():
        xs_ref[0:hist] = jnp.zeros((hist, wide), F32)
        h_ref[...] = jnp.zeros(h_ref.shape, F32)

    @pl.when(pl.program_id(0) > 0)
    def _():
        xs_ref[0:hist] = xs_ref[ts:ts + hist]

    xs_ref[hist:hist + ts] = x_ref[...]
    lam = lam_ref[...]
    softplus_neg_lam = jnp.maximum(-lam, 0.0) + jnp.log1p(jnp.exp(-jnp.abs(lam)))
    for b in seqs:
        sl = slice(b * w, (b + 1) * w)
        xc = cb_ref[...]
        for i in range(CONV_WIDTH):
            xc = xc + cw_ref[i:i + 1, :] * xs_ref[hist - pad + i:hist - pad + i + ts, sl]
        gates = _dot(xc.astype(BF16), wab_ref[...])
        rec_gate = _sigmoid(gates[:, :w] + ba_ref[...])
        in_gate = _sigmoid(gates[:, w:] + bx_ref[...])
        log_a = -LRU_C * rec_gate * softplus_neg_lam
        a = jnp.exp(log_a)
        a_ref[:, sl] = a
        one_minus_a2 = -jnp.tanh(log_a) * (a * a + 1.0)
        u_ref[:, sl] = jnp.sqrt(jnp.maximum(one_minus_a2, 0.0)) * (in_gate * xc)

    def step(i, h):
        h = a_ref[pl.ds(i, 1), :] * h + u_ref[pl.ds(i, 1), :]
        u_ref[pl.ds(i, 1), :] = h
        return h

    h_ref[...] = lax.fori_loop(0, ts, step, h_ref[...], unroll=8)

    for b in seqs:
        sl = slice(b * w, (b + 1) * w)
        g = gate_ref[:, sl]
        gelu = 0.5 * g * (1.0 + jnp.tanh(math.sqrt(2.0 / math.pi) * (g + 0.044715 * (g * g * g))))
        o_ref[:, sl] = _rms(u_ref[:, sl] * gelu, gn_ref[...])


def _rglru(x_t, gate_t, cw, cb, wab, ba, bx, lam, gn):
    s, wide = x_t.shape
    ts = LRU_TIME_TILE
    tile = pl.BlockSpec((ts, wide), lambda i: (i, 0))
    full = lambda a: pl.BlockSpec(a.shape, lambda i: (0,) * a.ndim)
    return pl.pallas_call(
        _lru_kernel,
        out_shape=jax.ShapeDtypeStruct((s, wide), F32),
        grid=(s // ts,),
        in_specs=[tile, tile, full(cw), full(cb), full(wab), full(ba), full(bx), full(lam), full(gn)],
        out_specs=tile,
        scratch_shapes=[pltpu.VMEM((ts + SUBLANES, wide), F32),
                        pltpu.VMEM((ts, wide), F32),
                        pltpu.VMEM((ts, wide), F32),
                        pltpu.VMEM((1, wide), F32)],
        compiler_params=_params("arbitrary"),
        name="rglru",
    )(x_t, gate_t, cw, cb, wab, ba, bx, lam, gn)


def _hg_constants():
    t = HG_CHUNK
    idx = np.arange(t)
    sums = [idx[:, None] >= idx[None, :]]
    masks = [idx[:, None] == idx[None, :]]
    for lev in range(HG_LEVELS):
        half = 1 << lev
        mid = (idx // (2 * half)) * (2 * half) + half
        upper = idx >= mid
        j = idx[None, :]
        up_rows = upper[:, None] & (j >= mid[:, None]) & (j <= idx[:, None])
        low_rows = (~upper)[:, None] & (j > idx[:, None]) & (j < mid[:, None])
        sums.append(up_rows | low_rows)
        same = (idx[:, None] // (2 * half)) == (idx[None, :] // (2 * half))
        masks.append(same & upper[:, None] & (~upper)[None, :])
    sum_mat = np.concatenate(sums, axis=0)
    sum_mat = jnp.asarray(np.concatenate([sum_mat, sum_mat], axis=1), BF16)
    mask = jnp.asarray(np.stack(masks, axis=0), F32)
    head = np.arange(GROUP_W) // HEAD_DIM
    same_head = head[:, None] == head[None, :]
    same_head = jnp.asarray(np.concatenate([same_head, same_head], axis=0), BF16)
    return sum_mat, mask, same_head


def _hg_kernel(x_ref, lbl_ref, ng_ref, sum_ref, mask_ref, same_ref, o_ref, st_ref, *, layer):
    t = HG_CHUNK
    w = GROUP_W

    seqs = range(x_ref.shape[0])

    @pl.when(pl.program_id(1) == 0)
    def _():
        st_ref[...] = jnp.zeros(st_ref.shape, F32)

    logits = lbl_ref[...]
    ex = jnp.exp(logits - jnp.max(logits, axis=0, keepdims=True))
    prob = ex / jnp.sum(ex, axis=0, keepdims=True)
    csum = prob[0:1, :]
    for i in range(1, layer + 1):
        csum = csum + prob[i:i + 1, :]
    lb = csum - prob[0:1, :]
    a_ = jnp.log(jnp.maximum(lb, LB_FLOOR))
    same = same_ref[...]
    lane_head = lax.broadcasted_iota(jnp.int32, (t, w), 1) // HEAD_DIM

    def by_head(x):
        return jnp.concatenate([jnp.where(lane_head == hh, x, jnp.zeros_like(x)) for hh in range(N_HEADS)], axis=0)

    q = [x_ref[b, :, 0:w] for b in seqs]
    vb = [x_ref[b, :, 2 * w:3 * w].astype(BF16) for b in seqs]
    key, sums = [], []
    for b in seqs:
        fz = x_ref[b, :, w:2 * w]
        b_ = jnp.log1p(-lb) + _log_sigmoid(fz)
        log_f = jnp.maximum(a_, b_) + jnp.log1p(jnp.exp(-jnp.abs(a_ - b_)))
        key.append((1.0 - lb) * _sigmoid(-fz))
        sums.append(_dot(sum_ref[...], jnp.concatenate(_split_bf16(log_f), axis=0)))

    out = []
    for b in seqs:
        cum = sums[b][0:t]
        last = cum[t - 1:t, :]
        st = st_ref[b]
        out.append(_dot_nt((q[b] * jnp.exp(cum)).astype(BF16), st.astype(BF16)))
        kdec = (key[b] * jnp.exp(last - cum)).astype(BF16)
        st_ref[b] = st * jnp.exp(last) + jnp.where(same[0:w] > 0, _dot_tn(vb[b], kdec), 0.0)

    scores = [[jnp.zeros((t, t), F32) for _ in range(N_HEADS)] for _ in seqs]
    for lev in range(HG_LEVELS + 1):
        for b in seqs:
            if lev == 0:
                qt, kt = q[b], key[b]
            else:
                decay = jnp.exp(sums[b][lev * t:(lev + 1) * t])
                qt, kt = q[b] * decay, key[b] * decay
            sc = _dot_nt(qt.astype(BF16), by_head(kt.astype(BF16)))
            for hh in range(N_HEADS):
                scores[b][hh] = scores[b][hh] + sc[:, hh * t:(hh + 1) * t] * mask_ref[lev]
    for b in seqs:
        all_scores = jnp.concatenate([s.astype(BF16) for s in scores[b]], axis=1)
        out[b] = out[b] + _dot(all_scores, by_head(vb[b]))

    mean_sq = [_dot(jnp.concatenate(_split_bf16(out[b] * out[b]), axis=1), same) * (1.0 / HEAD_DIM) for b in seqs]
    for b in seqs:
        g = x_ref[b, :, 3 * w:4 * w]
        normed = out[b] * lax.rsqrt(mean_sq[b] + NORM_EPS) * ng_ref[...]
        o_ref[b] = (normed * (g * _sigmoid(g))).astype(BF16)


def _hgrn2(x, lb_logits, ng, layer, batch):
    m = x.shape[0]
    t = HG_CHUNK
    nb = HG_SEQS
    seq = m // batch
    sum_mat, mask, same_head = _hg_constants()
    full = lambda a: pl.BlockSpec(a.shape, lambda b, c: (0,) * a.ndim)
    y = pl.pallas_call(
        functools.partial(_hg_kernel, layer=layer),
        out_shape=jax.ShapeDtypeStruct((batch, seq, GROUP_W), BF16),
        grid=(batch // nb, seq // t),
        in_specs=[pl.BlockSpec((nb, t, 4 * GROUP_W), lambda b, c: (b, c, 0)),
                  full(lb_logits), full(ng), full(sum_mat), full(mask), full(same_head)],
        out_specs=pl.BlockSpec((nb, t, GROUP_W), lambda b, c: (b, c, 0)),
        scratch_shapes=[pltpu.VMEM((nb, GROUP_W, GROUP_W), F32)],
        compiler_params=_params("arbitrary", "arbitrary"),
        name="hgrn2",
    )(x.reshape(batch, seq, 4 * GROUP_W), lb_logits, ng, sum_mat, mask, same_head)
    return y.reshape(m, GROUP_W)


def _out_ffn_kernel(h_ref, ya_ref, yb_ref, yc_ref, yd_ref, wout_ref, g2_ref, wg_ref, wu_ref, wd_ref,
                    gf_ref, o_ref, *, final):
    mixed = jnp.concatenate([ya_ref[...], yb_ref[...].astype(BF16), yc_ref[...], yd_ref[...]], axis=1)
    h1 = h_ref[...] + _dot(mixed, wout_ref[...])
    un = _rms(h1, g2_ref[...]).astype(BF16)
    chunks = [slice(c * FFN_CHUNK, (c + 1) * FFN_CHUNK) for c in range(FFN_HIDDEN // FFN_CHUNK)]
    gate_up = [(_dot(un, wg_ref[:, sl]), _dot(un, wu_ref[:, sl])) for sl in chunks]
    acc = h1
    for sl, (gt, up) in zip(chunks, gate_up):
        act = (gt * _sigmoid(gt) * up).astype(BF16)
        acc = acc + _dot(act, wd_ref[sl, :])
    if final:
        acc = _rms(acc, gf_ref[...])
    o_ref[...] = acc


def _out_ffn(h, ya, yb, yc, yd, wout, g2, wg, wu, wd, gf, layer, final):
    m = h.shape[0]
    tm = ROW_TILE
    rows = lambda w: pl.BlockSpec((tm, w), lambda i: (i, 0))

    def once(a):
        if a.ndim == 3:
            return pl.BlockSpec((None,) + a.shape[1:], lambda i: (layer, 0, 0), pipeline_mode=pl.Buffered(1))
        return pl.BlockSpec(a.shape, lambda i: (0, 0), pipeline_mode=pl.Buffered(1))
    tiles_per_seq = yb.shape[0] // tm
    yb_rows = pl.BlockSpec((tm, GROUP_W), lambda i: (i % tiles_per_seq, i // tiles_per_seq))
    return pl.pallas_call(
        functools.partial(_out_ffn_kernel, final=final),
        out_shape=jax.ShapeDtypeStruct((m, D_MODEL), F32),
        grid=(m // tm,),
        in_specs=[rows(D_MODEL), rows(GROUP_W), yb_rows, rows(GROUP_W), rows(GROUP_W),
                  once(wout), once(g2), once(wg), once(wu), once(wd), once(gf)],
        out_specs=rows(D_MODEL),
        compiler_params=_params("arbitrary"),
        name="out_ffn",
    )(h, ya, yb, yc, yd, wout, g2, wg, wu, wd, gf)


def _pad_in_proj(w):
    z = lambda n: jnp.zeros(w.shape[:-1] + (n,), w.dtype)
    sb_q = w[..., 928:928 + GROUP_W] * (LOG2_E * HEAD_DIM ** -0.5)
    return jnp.concatenate([w[..., :384], z(MLA_NOPE), w[..., 384:416], z(LANES - MLA_NOPE - MLA_ROPE),
                            w[..., 416:928], sb_q, w[..., 928 + GROUP_W:]], axis=-1).astype(BF16)


def _pad_heads(w, width):
    lead = w.shape[:-1]
    w = w.reshape(lead + (N_HEADS, width))
    w = jnp.pad(w, ((0, 0),) * (len(lead) + 1) + ((0, LANES - width),))
    return w.reshape(lead + (N_HEADS * LANES,))


def _block_diag(w):
    n, c, d = w.shape[-3:]
    eye = jnp.eye(n, dtype=w.dtype)
    return (eye[:, None, :, None] * w[..., :, :, None, :]).reshape(w.shape[:-3] + (n * c, n * d))


def kernel(x, positions, ln1_g, w_in, mla_q_norm_g, mla_w_uq, mla_kv_norm_g, mla_w_ukv, lru_conv_w, lru_conv_b, lru_w_a, lru_b_a, lru_w_x, lru_b_x, lru_lambda, hgrn_lb_logits, hgrn_norm_g, group_norm_g, w_out, ln2_g, w_ffn_gate, w_ffn_up, w_ffn_down, final_norm_g):
    batch, seq, d_model = x.shape
    depth = w_in.shape[0]
    m = batch * seq
    row = lambda v: v.reshape(1, -1)
    h = x.reshape(m, d_model)
    tabs = _rope_tables(positions)
    qk_scale = LOG2_E * (MLA_NOPE + MLA_ROPE) ** -0.5

    win = _pad_in_proj(w_in)
    wuq = _pad_heads(mla_w_uq * qk_scale, MLA_NOPE + MLA_ROPE).astype(BF16)
    wukv = mla_w_ukv.reshape(depth, MLA_KV_LORA, N_HEADS, MLA_NOPE + MLA_V)
    wuk = _pad_heads(wukv[..., :MLA_NOPE].reshape(depth, MLA_KV_LORA, -1), MLA_NOPE)
    wuv = wukv[..., MLA_NOPE:].reshape(depth, MLA_KV_LORA, -1)
    wukv = jnp.concatenate([wuk, wuv], axis=-1).astype(BF16)
    wab = jnp.concatenate([_block_diag(lru_w_a), _block_diag(lru_w_x)], axis=-1).astype(BF16)
    wout, wg, wu, wd = (w.astype(BF16) for w in (w_out, w_ffn_gate, w_ffn_up, w_ffn_down))

    for l in range(depth):
        qm, km, vm_t, lru_x, lru_gate, sb_qk, sb_vt, hg_in = _in_proj(
            h, row(ln1_g[l]), win, row(mla_q_norm_g[l]), wuq, row(mla_kv_norm_g[l]), wukv, tabs, l, batch)

        y_a = _mla_attention(qm, km, vm_t, row(group_norm_g[l, 0]), batch)
        y_b = _rglru(lru_x, lru_gate, lru_conv_w[l], row(lru_conv_b[l]), wab[l], row(lru_b_a[l]),
                     row(lru_b_x[l]), row(lru_lambda[l]), row(group_norm_g[l, 1]))

        y_c = _sb_attention(sb_qk, sb_vt, row(group_norm_g[l, 2]), batch)
        y_d = _hgrn2(hg_in, hgrn_lb_logits, row(hgrn_norm_g[l]), l, batch)

        h = _out_ffn(h, y_a, y_b, y_c, y_d, wout, row(ln2_g[l]), wg, wu, wd, row(final_norm_g),
                     l, final=(l == depth - 1))
    return h.reshape(batch, seq, d_model)
```

```python
import functools
import math

import numpy as np
import jax
import jax.numpy as jnp
from jax import lax
from jax.experimental import pallas as pl
from jax.experimental.pallas import tpu as pltpu

F32 = jnp.float32
BF16 = jnp.bfloat16

D_MODEL = 1024
N_HEADS = 4
MLA_Q_LORA = 256
MLA_KV_LORA = 128
MLA_NOPE = 64
MLA_ROPE = 32
MLA_V = 64
ROPE_THETA = 10000.0
LRU_WIDTH = 256
CONV_WIDTH = 4
LRU_C = 8.0
HEAD_DIM = 64
GROUP_W = 256
FFN_HIDDEN = 2816
NORM_EPS = 1e-6
MASK_VALUE = -1e30
LB_FLOOR = 1e-30

LANES = 128
SUBLANES = 8
VMEM_LIMIT_BYTES = 56 * 1024 * 1024

ROW_TILE = 512
ATTN_TILE = 256
SOFTMAX_ROWS = 64
SUM_ROWS = 16
PIPE_DEPTH = 2
ATTN_BODY_BLOCKS = 8
LOG2_E = math.log2(math.e)
LRU_TIME_TILE = 256
HG_CHUNK = 128
HG_LEVELS = int(math.log2(HG_CHUNK))
HG_SEQS = 4
FFN_CHUNK = 1408

IN_MLA = (0, 512)
IN_LRU = (512, 1024)
IN_SB = (1024, 1792)
IN_HG = (1792, 2816)
D_IN_PAD = 2816


def _params(*semantics):
    return pltpu.CompilerParams(dimension_semantics=semantics, vmem_limit_bytes=VMEM_LIMIT_BYTES)


def _rms(x, g):
    return x * lax.rsqrt(jnp.mean(x * x, axis=-1, keepdims=True) + NORM_EPS) * g


def _dot(a, b):
    return jnp.dot(a, b, preferred_element_type=F32)


def _dot_nt(a, b):
    return lax.dot_general(a, b, (((1,), (1,)), ((), ())), preferred_element_type=F32)


def _dot_tn(a, b):
    return lax.dot_general(a, b, (((0,), (0,)), ((), ())), preferred_element_type=F32)


def _split_bf16(x):
    hi = x.astype(BF16)
    lo = (x - hi.astype(F32)).astype(BF16)
    return hi, lo


def _sigmoid(x):
    return 1.0 / (1.0 + jnp.exp(-x))


def _log_sigmoid(x):
    return jnp.minimum(x, 0.0) - jnp.log1p(jnp.exp(-jnp.abs(x)))


def _rope_table_kernel(pos_ref, freq_ref, m1_ref, m2_ref, c_ref, s1_ref, s2_ref):
    ang = pos_ref[...] * freq_ref[...]
    sn = jnp.sin(ang)
    c_ref[...] = jnp.cos(ang)
    s1_ref[...] = sn * m1_ref[...]
    s2_ref[...] = sn * m2_ref[...]


def _rope_tables(positions):
    m = positions.size
    half = MLA_ROPE // 2
    inv_freq = ROPE_THETA ** (-jnp.arange(half, dtype=F32) / half)
    zeros = jnp.zeros((MLA_NOPE,), F32)
    tail = jnp.zeros((LANES - MLA_NOPE - MLA_ROPE,), F32)
    freq = jnp.concatenate([zeros, inv_freq, inv_freq, tail]).reshape(1, LANES)
    ones = jnp.ones((half,), F32)
    m1 = jnp.concatenate([zeros, -ones, 0 * ones, tail]).reshape(1, LANES)
    m2 = jnp.concatenate([zeros, 0 * ones, ones, tail]).reshape(1, LANES)
    pos = positions.astype(F32).reshape(m, 1)
    tm = ROW_TILE
    row = pl.BlockSpec((1, LANES), lambda i: (0, 0))
    tab = pl.BlockSpec((tm, LANES), lambda i: (i, 0))
    shape = jax.ShapeDtypeStruct((m, LANES), F32)
    return pl.pallas_call(
        _rope_table_kernel,
        out_shape=(shape, shape, shape),
        grid=(m // tm,),
        in_specs=[pl.BlockSpec((tm, 1), lambda i: (i, 0)), row, row, row],
        out_specs=(tab, tab, tab),
        compiler_params=_params("arbitrary"),
        name="rope_tables",
    )(pos, freq, m1, m2)


def _in_proj_kernel(h_ref, g1_ref, win_ref, qg_ref, wuq_ref, kvg_ref, wukv_ref,
                    c_ref, s1_ref, s2_ref,
                    qm_ref, km_ref, vmt_ref, lrux_ref, lrug_ref, sb_ref, sbvt_ref, hg_ref):
    t = ATTN_TILE
    n_blocks = h_ref.shape[0] // t

    def store_transposed(ref, v):
        for blk in range(n_blocks):
            ref[blk] = v[blk * t:(blk + 1) * t, :].T.astype(BF16)

    u = _rms(h_ref[...], g1_ref[...]).astype(BF16)
    p = _dot(u, win_ref[:, IN_MLA[0]:IN_MLA[1]])
    lrux_ref[...] = _dot(u, win_ref[:, IN_LRU[0]:IN_LRU[0] + LRU_WIDTH])
    lrug_ref[...] = _dot(u, win_ref[:, IN_LRU[0] + LRU_WIDTH:IN_LRU[1]])
    cq = _rms(p[:, 0:MLA_Q_LORA], qg_ref[...]).astype(BF16)
    ckv = _rms(p[:, MLA_Q_LORA:MLA_Q_LORA + MLA_KV_LORA], kvg_ref[...]).astype(BF16)
    q = _dot(cq, wuq_ref[...])
    kv = _dot(ckv, wukv_ref[...])
    sb = _dot(u, win_ref[:, IN_SB[0]:IN_SB[1]])
    c, s1, s2 = c_ref[...], s1_ref[...], s2_ref[...]

    def rope(x):
        return x * c + pltpu.roll(x, LANES - MLA_ROPE // 2, 1) * s1 + pltpu.roll(x, MLA_ROPE // 2, 1) * s2

    kpe = rope(p[:, 384:512])
    for hh in range(N_HEADS):
        sl = slice(hh * LANES, (hh + 1) * LANES)
        qm_ref[:, sl] = rope(q[:, sl]).astype(BF16)
        km_ref[:, sl] = (kv[:, sl] + kpe).astype(BF16)
    store_transposed(vmt_ref, kv[:, N_HEADS * LANES:])
    hg_ref[...] = _dot(u, win_ref[:, IN_HG[0]:IN_HG[1]])
    sb_ref[...] = sb[:, :2 * GROUP_W].astype(BF16)
    store_transposed(sbvt_ref, sb[:, 2 * GROUP_W:])


def _in_proj(h, g1, win, qg, wuq, kvg, wukv, tabs, layer, batch):
    m = h.shape[0]
    seq = m // batch
    tm = ROW_TILE
    t = ATTN_TILE
    rows = lambda w: pl.BlockSpec((tm, w), lambda i: (i, 0))

    def full(a):
        if a.ndim == 3:
            return pl.BlockSpec((None,) + a.shape[1:], lambda i: (layer, 0, 0))
        return pl.BlockSpec(a.shape, lambda i: (0, 0))
    c, s1, s2 = tabs
    v_t = jax.ShapeDtypeStruct((m // t, GROUP_W, t), BF16)
    out_shape = (
        jax.ShapeDtypeStruct((m, N_HEADS * LANES), BF16),
        jax.ShapeDtypeStruct((m, N_HEADS * LANES), BF16),
        v_t,
        jax.ShapeDtypeStruct((seq, batch * LRU_WIDTH), F32),
        jax.ShapeDtypeStruct((seq, batch * LRU_WIDTH), F32),
        jax.ShapeDtypeStruct((m, 2 * GROUP_W), BF16),
        v_t,
        jax.ShapeDtypeStruct((m, 4 * GROUP_W), F32),
    )
    tiles_per_seq = seq // tm

    def out_spec(sd):
        if sd.shape[0] == seq:
            return pl.BlockSpec((tm, LRU_WIDTH), lambda i: (i % tiles_per_seq, i // tiles_per_seq))
        if len(sd.shape) == 3:
            return pl.BlockSpec((tm // t, GROUP_W, t), lambda i: (i, 0, 0))
        return rows(sd.shape[1])
    return pl.pallas_call(
        _in_proj_kernel,
        out_shape=out_shape,
        grid=(m // tm,),
        in_specs=[rows(D_MODEL), full(g1), full(win), full(qg), full(wuq), full(kvg), full(wukv),
                  rows(LANES), rows(LANES), rows(LANES)],
        out_specs=tuple(out_spec(sd) for sd in out_shape),
        compiler_params=_params("arbitrary"),
        name="in_proj",
    )(h, g1, win, qg, wuq, kvg, wukv, c, s1, s2)


def _pair_order(nq):
    dummy = (nq, 0)
    pad = lambda pairs: pairs + [dummy] * (-len(pairs) % ATTN_BODY_BLOCKS)
    diag = pad([(i, i) for i in range(nq)])
    off = pad([(i, i - d) for d in range(1, nq) for i in range(d, nq)])
    order = np.array([dummy] + diag + off + [dummy] * 2, np.int32)
    return (jnp.asarray(order[:, 0]), jnp.asarray(order[:, 1]),
            len(diag) // ATTN_BODY_BLOCKS, len(off) // ATTN_BODY_BLOCKS)


def _mla_kernel(qtab_ref, ktab_ref, q_ref, k_ref, vt_ref, gn_ref, o_ref,
                m_ref, acc_ref, s0_ref, s1_ref, p_ref, alpha_ref, *, diag_trips, off_trips):
    t = ATTN_TILE
    nq = q_ref.shape[0] // t
    m_ref[...] = jnp.full(m_ref.shape, MASK_VALUE, F32)
    acc_ref[...] = jnp.zeros(acc_ref.shape, F32)
    n_chunks = t // SOFTMAX_ROWS
    key_idx = lax.broadcasted_iota(jnp.int32, (SOFTMAX_ROWS, t), 0)
    qry_idx = lax.broadcasted_iota(jnp.int32, (SOFTMAX_ROWS, t), 1)
    ones_rows = jnp.ones((SUM_ROWS, t), BF16)
    s_refs = (s0_ref, s1_ref)

    def scores(n, hh, s_ref):
        q0 = pl.multiple_of(jnp.minimum(qtab_ref[n], nq - 1) * t, t)
        k0 = pl.multiple_of(ktab_ref[n] * t, t)
        sl = slice(hh * LANES, (hh + 1) * LANES)
        s_ref[hh] = _dot_nt(k_ref[pl.ds(k0, t), sl], q_ref[pl.ds(q0, t), sl])

    def softmax(n, hh, s_ref, diagonal):
        def chunk(c):
            s = s_ref[hh, c * SOFTMAX_ROWS:(c + 1) * SOFTMAX_ROWS, :]
            if diagonal:
                s = jnp.where(key_idx + c * SOFTMAX_ROWS <= qry_idx, s, MASK_VALUE)
            return s

        qb = qtab_ref[n]
        m_prev = m_ref[qb, hh]
        m_blk = chunk(0)
        for c in range(1, n_chunks):
            m_blk = jnp.maximum(m_blk, chunk(c))
        m_new = jnp.maximum(m_prev, jnp.max(m_blk, axis=0, keepdims=True))
        alpha_ref[hh] = jnp.exp2(m_prev - m_new)
        for c in range(n_chunks):
            p_ref[hh, c * SOFTMAX_ROWS:(c + 1) * SOFTMAX_ROWS, :] = jnp.exp2(chunk(c) - m_new).astype(BF16)
        m_ref[qb, hh] = m_new

    def weighted_values(n, hh):
        qb = qtab_ref[n]
        v_ext = jnp.concatenate([vt_ref[ktab_ref[n], hh * MLA_V:(hh + 1) * MLA_V, :], ones_rows], axis=0)
        acc_ref[qb, hh] = alpha_ref[hh] * acc_ref[qb, hh] + _dot(v_ext, p_ref[hh])

    def sweep(first, trips, diagonal):
        def body(r, carry):
            start = first + ATTN_BODY_BLOCKS * r
            previous = (jnp.where(r == 0, 0, start - 1), N_HEADS - 1)
            for i in range(ATTN_BODY_BLOCKS):
                for hh in range(N_HEADS):
                    scores(start + i + 1, hh, s_refs[(i + 1) % 2])
                    weighted_values(*previous)
                    softmax(start + i, hh, s_refs[i % 2], diagonal)
                    previous = (start + i, hh)
            return carry

        lax.fori_loop(0, trips, body, 0)
        weighted_values(first + ATTN_BODY_BLOCKS * trips - 1, N_HEADS - 1)

    alpha_ref[...] = jnp.zeros(alpha_ref.shape, F32)
    p_ref[...] = jnp.zeros(p_ref.shape, BF16)
    for hh in range(N_HEADS):
        scores(1, hh, s0_ref)
    sweep(1, diag_trips, True)
    sweep(1 + ATTN_BODY_BLOCKS * diag_trips, off_trips, False)

    def write_out(i, carry):
        r0 = pl.multiple_of(i * t, t)
        y_t = jnp.concatenate([acc_ref[i, hh, 0:MLA_V, :] / acc_ref[i, hh, MLA_V:MLA_V + 1, :]
                               for hh in range(N_HEADS)], axis=0)
        o_ref[pl.ds(r0, t), :] = _rms(y_t.T, gn_ref[...]).astype(BF16)
        return carry

    lax.fori_loop(0, nq, write_out, 0)


def _mla_attention(q, k, v_t, gn, batch):
    m = q.shape[0]
    s = m // batch
    t = ATTN_TILE
    nq = s // t
    qtab, ktab, diag_trips, off_trips = _pair_order(nq)
    smem = pl.BlockSpec(memory_space=pltpu.SMEM)
    return pl.pallas_call(
        functools.partial(_mla_kernel, diag_trips=diag_trips, off_trips=off_trips),
        out_shape=jax.ShapeDtypeStruct((m, GROUP_W), BF16),
        grid=(batch,),
        in_specs=[smem, smem,
                  pl.BlockSpec((s, N_HEADS * LANES), lambda b: (b, 0)),
                  pl.BlockSpec((s, N_HEADS * LANES), lambda b: (b, 0)),
                  pl.BlockSpec((nq, GROUP_W, t), lambda b: (b, 0, 0)),
                  pl.BlockSpec((1, GROUP_W), lambda b: (0, 0))],
        out_specs=pl.BlockSpec((s, GROUP_W), lambda b: (b, 0)),
        scratch_shapes=[pltpu.VMEM((nq + 1, N_HEADS, 1, t), F32),
                        pltpu.VMEM((nq + 1, N_HEADS, MLA_V + SUM_ROWS, t), F32),
                        pltpu.VMEM((N_HEADS, t, t), F32),
                        pltpu.VMEM((N_HEADS, t, t), F32),
                        pltpu.VMEM((N_HEADS, t, t), BF16),
                        pltpu.VMEM((N_HEADS, 1, t), F32)],
        compiler_params=_params("arbitrary"),
        name="mla_attention",
    )(qtab, ktab, q, k, v_t, gn)


def _sb_kernel(qtab_ref, ktab_ref, q_ref, k_ref, vt_ref, tri_ref, gn_ref, o_ref,
               qm_ref, carry_ref, acc_ref, z0_ref, z1_ref, lb_ref, lk_ref, later_ref, w_ref,
               *, diag_trips, off_trips):
    t = ATTN_TILE
    nq = q_ref.shape[0] // t
    n_chunks = t // SOFTMAX_ROWS
    lane = lax.broadcasted_iota(jnp.int32, (t, LANES), 1)

    def mask_queries(i, carry):
        r0 = pl.multiple_of(i * t, t)
        for hh in range(N_HEADS):
            qp = q_ref[pl.ds(r0, t), (hh // 2) * LANES:(hh // 2 + 1) * LANES]
            own = (lane >= (hh % 2) * HEAD_DIM) & (lane < (hh % 2 + 1) * HEAD_DIM)
            qm_ref[hh, pl.ds(r0, t), :] = jnp.where(own, qp, jnp.zeros_like(qp))
        return carry

    lax.fori_loop(0, nq, mask_queries, 0)
    qm_ref[:, nq * t:(nq + 1) * t, :] = jnp.zeros((N_HEADS, t, LANES), BF16)
    carry_ref[...] = jnp.zeros(carry_ref.shape, F32)
    acc_ref[...] = jnp.zeros(acc_ref.shape, F32)
    lb_ref[...] = jnp.zeros(lb_ref.shape, F32)
    later_ref[...] = jnp.zeros(later_ref.shape, F32)
    key_idx = lax.broadcasted_iota(jnp.int32, (SOFTMAX_ROWS, t), 0)
    qry_idx = lax.broadcasted_iota(jnp.int32, (SOFTMAX_ROWS, t), 1)

    def rows(c):
        return slice(c * SOFTMAX_ROWS, (c + 1) * SOFTMAX_ROWS)

    def logits(n, hh, z_ref):
        q0 = pl.multiple_of(qtab_ref[n] * t, t)
        k0 = pl.multiple_of(ktab_ref[n] * t, t)
        ps = slice((hh // 2) * LANES, (hh // 2 + 1) * LANES)
        z_ref[hh] = _dot_nt(k_ref[pl.ds(k0, t), ps], qm_ref[hh, pl.ds(q0, t), :])

    def log_terms(hh, z_ref, diagonal):
        for c in range(n_chunks):
            z = z_ref[hh, rows(c), :]
            softplus = jnp.log2(1.0 + jnp.exp2(-jnp.abs(z)))
            log_beta = jnp.minimum(z, 0.0) - softplus
            log_keep = log_beta - z
            if diagonal:
                log_keep = jnp.where(key_idx + c * SOFTMAX_ROWS < qry_idx, log_keep, 0.0)
            lb_ref[hh, rows(c), :] = log_beta
            lk_ref[hh, rows(c), :] = log_keep.astype(BF16)
        later_ref[hh] = _dot(tri_ref[...], lk_ref[hh])

    def weights_pv(n, hh, diagonal):
        qb = qtab_ref[n]
        carry = carry_ref[qb, hh]
        for c in range(n_chunks):
            w = jnp.exp2(lb_ref[hh, rows(c), :] + later_ref[hh, rows(c), :] + carry)
            if diagonal:
                w = jnp.where(key_idx + c * SOFTMAX_ROWS < qry_idx, w, 0.0)
            w_ref[hh, rows(c), :] = w.astype(BF16)
        carry_ref[qb, hh] = carry + later_ref[hh, t:t + 1, :]
        v_t = vt_ref[ktab_ref[n], hh * HEAD_DIM:(hh + 1) * HEAD_DIM, :]
        acc_ref[qb, hh] += _dot(v_t, w_ref[hh])

    tail_heads = range(N_HEADS - PIPE_DEPTH, N_HEADS)
    z_refs = (z0_ref, z1_ref)

    def trip(first, resumed, diagonal):
        pending = [(resumed, hh, diagonal) for hh in tail_heads]
        for i in range(ATTN_BODY_BLOCKS):
            for hh in range(N_HEADS):
                ahead = hh + PIPE_DEPTH
                if ahead < N_HEADS:
                    logits(first + i, ahead, z_refs[i % 2])
                else:
                    logits(first + i + 1, ahead - N_HEADS, z_refs[(i + 1) % 2])
                weights_pv(*pending.pop(0))
                log_terms(hh, z_refs[i % 2], diagonal)
                pending.append((first + i, hh, diagonal))

    def sweep(first, trips, diagonal):
        def body(r, carry):
            start = first + ATTN_BODY_BLOCKS * r
            trip(start, jnp.where(r == 0, 0, start - 1), diagonal)
            return carry

        lax.fori_loop(0, trips, body, 0)
        last = first + ATTN_BODY_BLOCKS * trips - 1
        for hh in tail_heads:
            weights_pv(last, hh, diagonal)

    for hh in range(PIPE_DEPTH):
        logits(1, hh, z0_ref)
    sweep(1, diag_trips, True)
    sweep(1 + ATTN_BODY_BLOCKS * diag_trips, off_trips, False)

    def write_out(i, carry):
        r0 = pl.multiple_of(i * t, t)
        y_t = jnp.concatenate([acc_ref[i, hh] for hh in range(N_HEADS)], axis=0)
        o_ref[pl.ds(r0, t), :] = _rms(y_t.T, gn_ref[...]).astype(BF16)
        return carry

    lax.fori_loop(0, nq, write_out, 0)


def _sb_attention(qk, v_t, gn, batch):
    m = qk.shape[0]
    s = m // batch
    t = ATTN_TILE
    nq = s // t
    idx = np.arange(t)
    later = idx[None, :] > idx[:, None]
    total = np.ones((SUM_ROWS, t), bool)
    tri = jnp.asarray(np.concatenate([later, total], axis=0), BF16)
    qtab, ktab, diag_trips, off_trips = _pair_order(nq)
    smem = pl.BlockSpec(memory_space=pltpu.SMEM)
    return pl.pallas_call(
        functools.partial(_sb_kernel, diag_trips=diag_trips, off_trips=off_trips),
        out_shape=jax.ShapeDtypeStruct((m, GROUP_W), BF16),
        grid=(batch,),
        in_specs=[smem, smem,
                  pl.BlockSpec((s, GROUP_W), lambda b: (b, 0)),
                  pl.BlockSpec((s, GROUP_W), lambda b: (b, 1)),
                  pl.BlockSpec((nq, GROUP_W, t), lambda b: (b, 0, 0)),
                  pl.BlockSpec(tri.shape, lambda b: (0, 0)),
                  pl.BlockSpec((1, GROUP_W), lambda b: (0, 0))],
        out_specs=pl.BlockSpec((s, GROUP_W), lambda b: (b, 0)),
        scratch_shapes=[pltpu.VMEM((N_HEADS, (nq + 1) * t, LANES), BF16),
                        pltpu.VMEM((nq + 1, N_HEADS, 1, t), F32),
                        pltpu.VMEM((nq + 1, N_HEADS, HEAD_DIM, t), F32),
                        pltpu.VMEM((N_HEADS, t, t), F32),
                        pltpu.VMEM((N_HEADS, t, t), F32),
                        pltpu.VMEM((N_HEADS, t, t), F32),
                        pltpu.VMEM((N_HEADS, t, t), BF16),
                        pltpu.VMEM((N_HEADS, t + SUM_ROWS, t), F32),
                        pltpu.VMEM((N_HEADS, t, t), BF16)],
        compiler_params=_params("arbitrary"),
        name="sb_attention",
    )(qtab, ktab, qk, qk, v_t, tri, gn)


def _lru_kernel(x_ref, gate_ref, cw_ref, cb_ref, wab_ref, ba_ref, bx_ref, lam_ref, gn_ref,
                o_ref, xs_ref, a_ref, u_ref, h_ref):
    ts, wide = x_ref.shape
    w = LRU_WIDTH
    seqs = range(wide // w)
    pad = CONV_WIDTH - 1
    hist = SUBLANES

    @pl.when(pl.program_id(0) == 0)
    def _():
        xs_ref[0:hist] = jnp.zeros((hist, wide), F32)
        h_ref[...] = jnp.zeros(h_ref.shape, F32)

    @pl.when(pl.program_id(0) > 0)
    def _():
        xs_ref[0:hist] = xs_ref[ts:ts + hist]

    xs_ref[hist:hist + ts] = x_ref[...]
    lam = lam_ref[...]
    softplus_neg_lam = jnp.maximum(-lam, 0.0) + jnp.log1p(jnp.exp(-jnp.abs(lam)))
    for b in seqs:
        sl = slice(b * w, (b + 1) * w)
        xc = cb_ref[...]
        for i in range(CONV_WIDTH):
            xc = xc + cw_ref[i:i + 1, :] * xs_ref[hist - pad + i:hist - pad + i + ts, sl]
        gates = _dot(xc.astype(BF16), wab_ref[...])
        rec_gate = _sigmoid(gates[:, :w] + ba_ref[...])
        in_gate = _sigmoid(gates[:, w:] + bx_ref[...])
        log_a = -LRU_C * rec_gate * softplus_neg_lam
        a = jnp.exp(log_a)
        a_ref[:, sl] = a
        one_minus_a2 = -jnp.tanh(log_a) * (a * a + 1.0)
        u_ref[:, sl] = jnp.sqrt(jnp.maximum(one_minus_a2, 0.0)) * (in_gate * xc)

    def step(i, h):
        h = a_ref[pl.ds(i, 1), :] * h + u_ref[pl.ds(i, 1), :]
        u_ref[pl.ds(i, 1), :] = h
        return h

    h_ref[...] = lax.fori_loop(0, ts, step, h_ref[...], unroll=8)

    for b in seqs:
        sl = slice(b * w, (b + 1) * w)
        g = gate_ref[:, sl]
        gelu = 0.5 * g * (1.0 + jnp.tanh(math.sqrt(2.0 / math.pi) * (g + 0.044715 * (g * g * g))))
        o_ref[:, sl] = _rms(u_ref[:, sl] * gelu, gn_ref[...])


def _rglru(x_t, gate_t, cw, cb, wab, ba, bx, lam, gn):
    s, wide = x_t.shape
    ts = LRU_TIME_TILE
    tile = pl.BlockSpec((ts, wide), lambda i: (i, 0))
    full = lambda a: pl.BlockSpec(a.shape, lambda i: (0,) * a.ndim)
    return pl.pallas_call(
        _lru_kernel,
        out_shape=jax.ShapeDtypeStruct((s, wide), F32),
        grid=(s // ts,),
        in_specs=[tile, tile, full(cw), full(cb), full(wab), full(ba), full(bx), full(lam), full(gn)],
        out_specs=tile,
        scratch_shapes=[pltpu.VMEM((ts + SUBLANES, wide), F32),
                        pltpu.VMEM((ts, wide), F32),
                        pltpu.VMEM((ts, wide), F32),
                        pltpu.VMEM((1, wide), F32)],
        compiler_params=_params("arbitrary"),
        name="rglru",
    )(x_t, gate_t, cw, cb, wab, ba, bx, lam, gn)


def _hg_constants():
    t = HG_CHUNK
    idx = np.arange(t)
    sums = [idx[:, None] >= idx[None, :]]
    masks = [idx[:, None] == idx[None, :]]
    for lev in range(HG_LEVELS):
        half = 1 << lev
        mid = (idx // (2 * half)) * (2 * half) + half
        upper = idx >= mid
        j = idx[None, :]
        up_rows = upper[:, None] & (j >= mid[:, None]) & (j <= idx[:, None])
        low_rows = (~upper)[:, None] & (j > idx[:, None]) & (j < mid[:, None])
        sums.append(up_rows | low_rows)
        same = (idx[:, None] // (2 * half)) == (idx[None, :] // (2 * half))
        masks.append(same & upper[:, None] & (~upper)[None, :])
    sum_mat = np.concatenate(sums, axis=0)
    sum_mat = jnp.asarray(np.concatenate([sum_mat, sum_mat], axis=1), BF16)
    mask = jnp.asarray(np.stack(masks, axis=0), F32)
    head = np.arange(GROUP_W) // HEAD_DIM
    same_head = head[:, None] == head[None, :]
    same_head = jnp.asarray(np.concatenate([same_head, same_head], axis=0), BF16)
    return sum_mat, mask, same_head


def _hg_kernel(x_ref, lbl_ref, ng_ref, sum_ref, mask_ref, same_ref, o_ref, st_ref, *, layer):
    t = HG_CHUNK
    w = GROUP_W

    seqs = range(x_ref.shape[0])

    @pl.when(pl.program_id(1) == 0)
    def _():
        st_ref[...] = jnp.zeros(st_ref.shape, F32)

    logits = lbl_ref[...]
    ex = jnp.exp(logits - jnp.max(logits, axis=0, keepdims=True))
    prob = ex / jnp.sum(ex, axis=0, keepdims=True)
    csum = prob[0:1, :]
    for i in range(1, layer + 1):
        csum = csum + prob[i:i + 1, :]
    lb = csum - prob[0:1, :]
    a_ = jnp.log(jnp.maximum(lb, LB_FLOOR))
    same = same_ref[...]
    lane_head = lax.broadcasted_iota(jnp.int32, (t, w), 1) // HEAD_DIM

    def by_head(x):
        return jnp.concatenate([jnp.where(lane_head == hh, x, jnp.zeros_like(x)) for hh in range(N_HEADS)], axis=0)

    q = [x_ref[b, :, 0:w] for b in seqs]
    vb = [x_ref[b, :, 2 * w:3 * w].astype(BF16) for b in seqs]
    key, sums = [], []
    for b in seqs:
        fz = x_ref[b, :, w:2 * w]
        b_ = jnp.log1p(-lb) + _log_sigmoid(fz)
        log_f = jnp.maximum(a_, b_) + jnp.log1p(jnp.exp(-jnp.abs(a_ - b_)))
        key.append((1.0 - lb) * _sigmoid(-fz))
        sums.append(_dot(sum_ref[...], jnp.concatenate(_split_bf16(log_f), axis=0)))

    out = []
    for b in seqs:
        cum = sums[b][0:t]
        last = cum[t - 1:t, :]
        st = st_ref[b]
        out.append(_dot_nt((q[b] * jnp.exp(cum)).astype(BF16), st.astype(BF16)))
        kdec = (key[b] * jnp.exp(last - cum)).astype(BF16)
        st_ref[b] = st * jnp.exp(last) + jnp.where(same[0:w] > 0, _dot_tn(vb[b], kdec), 0.0)

    scores = [[jnp.zeros((t, t), F32) for _ in range(N_HEADS)] for _ in seqs]
    for lev in range(HG_LEVELS + 1):
        for b in seqs:
            if lev == 0:
                qt, kt = q[b], key[b]
            else:
                decay = jnp.exp(sums[b][lev * t:(lev + 1) * t])
                qt, kt = q[b] * decay, key[b] * decay
            sc = _dot_nt(qt.astype(BF16), by_head(kt.astype(BF16)))
            for hh in range(N_HEADS):
                scores[b][hh] = scores[b][hh] + sc[:, hh * t:(hh + 1) * t] * mask_ref[lev]
    for b in seqs:
        all_scores = jnp.concatenate([s.astype(BF16) for s in scores[b]], axis=1)
        out[b] = out[b] + _dot(all_scores, by_head(vb[b]))

    mean_sq = [_dot(jnp.concatenate(_split_bf16(out[b] * out[b]), axis=1), same) * (1.0 / HEAD_DIM) for b in seqs]
    for b in seqs:
        g = x_ref[b, :, 3 * w:4 * w]
        normed = out[b] * lax.rsqrt(mean_sq[b] + NORM_EPS) * ng_ref[...]
        o_ref[b] = (normed * (g * _sigmoid(g))).astype(BF16)


def _hgrn2(x, lb_logits, ng, layer, batch):
    m = x.shape[0]
    t = HG_CHUNK
    nb = HG_SEQS
    seq = m // batch
    sum_mat, mask, same_head = _hg_constants()
    full = lambda a: pl.BlockSpec(a.shape, lambda b, c: (0,) * a.ndim)
    y = pl.pallas_call(
        functools.partial(_hg_kernel, layer=layer),
        out_shape=jax.ShapeDtypeStruct((batch, seq, GROUP_W), BF16),
        grid=(batch // nb, seq // t),
        in_specs=[pl.BlockSpec((nb, t, 4 * GROUP_W), lambda b, c: (b, c, 0)),
                  full(lb_logits), full(ng), full(sum_mat), full(mask), full(same_head)],
        out_specs=pl.BlockSpec((nb, t, GROUP_W), lambda b, c: (b, c, 0)),
        scratch_shapes=[pltpu.VMEM((nb, GROUP_W, GROUP_W), F32)],
        compiler_params=_params("arbitrary", "arbitrary"),
        name="hgrn2",
    )(x.reshape(batch, seq, 4 * GROUP_W), lb_logits, ng, sum_mat, mask, same_head)
    return y.reshape(m, GROUP_W)


def _out_ffn_kernel(h_ref, ya_ref, yb_ref, yc_ref, yd_ref, wout_ref, g2_ref, wg_ref, wu_ref, wd_ref,
                    gf_ref, o_ref, *, final):
    mixed = jnp.concatenate([ya_ref[...], yb_ref[...].astype(BF16), yc_ref[...], yd_ref[...]], axis=1)
    h1 = h_ref[...] + _dot(mixed, wout_ref[...])
    un = _rms(h1, g2_ref[...]).astype(BF16)
    chunks = [slice(c * FFN_CHUNK, (c + 1) * FFN_CHUNK) for c in range(FFN_HIDDEN // FFN_CHUNK)]
    gate_up = [(_dot(un, wg_ref[:, sl]), _dot(un, wu_ref[:, sl])) for sl in chunks]
    acc = h1
    for sl, (gt, up) in zip(chunks, gate_up):
        act = (gt * _sigmoid(gt) * up).astype(BF16)
        acc = acc + _dot(act, wd_ref[sl, :])
    if final:
        acc = _rms(acc, gf_ref[...])
    o_ref[...] = acc


def _out_ffn(h, ya, yb, yc, yd, wout, g2, wg, wu, wd, gf, layer, final):
    m = h.shape[0]
    tm = ROW_TILE
    rows = lambda w: pl.BlockSpec((tm, w), lambda i: (i, 0))

    def once(a):
        if a.ndim == 3:
            return pl.BlockSpec((None,) + a.shape[1:], lambda i: (layer, 0, 0), pipeline_mode=pl.Buffered(1))
        return pl.BlockSpec(a.shape, lambda i: (0, 0), pipeline_mode=pl.Buffered(1))
    tiles_per_seq = yb.shape[0] // tm
    yb_rows = pl.BlockSpec((tm, GROUP_W), lambda i: (i % tiles_per_seq, i // tiles_per_seq))
    return pl.pallas_call(
        functools.partial(_out_ffn_kernel, final=final),
        out_shape=jax.ShapeDtypeStruct((m, D_MODEL), F32),
        grid=(m // tm,),
        in_specs=[rows(D_MODEL), rows(GROUP_W), yb_rows, rows(GROUP_W), rows(GROUP_W),
                  once(wout), once(g2), once(wg), once(wu), once(wd), once(gf)],
        out_specs=rows(D_MODEL),
        compiler_params=_params("arbitrary"),
        name="out_ffn",
    )(h, ya, yb, yc, yd, wout, g2, wg, wu, wd, gf)


def _pad_in_proj(w):
    z = lambda n: jnp.zeros(w.shape[:-1] + (n,), w.dtype)
    sb_q = w[..., 928:928 + GROUP_W] * (LOG2_E * HEAD_DIM ** -0.5)
    return jnp.concatenate([w[..., :384], z(MLA_NOPE), w[..., 384:416], z(LANES - MLA_NOPE - MLA_ROPE),
                            w[..., 416:928], sb_q, w[..., 928 + GROUP_W:]], axis=-1).astype(BF16)


def _pad_heads(w, width):
    lead = w.shape[:-1]
    w = w.reshape(lead + (N_HEADS, width))
    w = jnp.pad(w, ((0, 0),) * (len(lead) + 1) + ((0, LANES - width),))
    return w.reshape(lead + (N_HEADS * LANES,))


def _block_diag(w):
    n, c, d = w.shape[-3:]
    eye = jnp.eye(n, dtype=w.dtype)
    return (eye[:, None, :, None] * w[..., :, :, None, :]).reshape(w.shape[:-3] + (n * c, n * d))


def kernel(x, positions, ln1_g, w_in, mla_q_norm_g, mla_w_uq, mla_kv_norm_g, mla_w_ukv, lru_conv_w, lru_conv_b, lru_w_a, lru_b_a, lru_w_x, lru_b_x, lru_lambda, hgrn_lb_logits, hgrn_norm_g, group_norm_g, w_out, ln2_g, w_ffn_gate, w_ffn_up, w_ffn_down, final_norm_g):
    batch, seq, d_model = x.shape
    depth = w_in.shape[0]
    m = batch * seq
    row = lambda v: v.reshape(1, -1)
    h = x.reshape(m, d_model)
    tabs = _rope_tables(positions)
    qk_scale = LOG2_E * (MLA_NOPE + MLA_ROPE) ** -0.5

    win = _pad_in_proj(w_in)
    wuq = _pad_heads(mla_w_uq * qk_scale, MLA_NOPE + MLA_ROPE).astype(BF16)
    wukv = mla_w_ukv.reshape(depth, MLA_KV_LORA, N_HEADS, MLA_NOPE + MLA_V)
    wuk = _pad_heads(wukv[..., :MLA_NOPE].reshape(depth, MLA_KV_LORA, -1), MLA_NOPE)
    wuv = wukv[..., MLA_NOPE:].reshape(depth, MLA_KV_LORA, -1)
    wukv = jnp.concatenate([wuk, wuv], axis=-1).astype(BF16)
    wab = jnp.concatenate([_block_diag(lru_w_a), _block_diag(lru_w_x)], axis=-1).astype(BF16)
    wout, wg, wu, wd = (w.astype(BF16) for w in (w_out, w_ffn_gate, w_ffn_up, w_ffn_down))

    for l in range(depth):
        qm, km, vm_t, lru_x, lru_gate, sb_qk, sb_vt, hg_in = _in_proj(
            h, row(ln1_g[l]), win, row(mla_q_norm_g[l]), wuq, row(mla_kv_norm_g[l]), wukv, tabs, l, batch)

        y_a = _mla_attention(qm, km, vm_t, row(group_norm_g[l, 0]), batch)
        y_b = _rglru(lru_x, lru_gate, lru_conv_w[l], row(lru_conv_b[l]), wab[l], row(lru_b_a[l]),
                     row(lru_b_x[l]), row(lru_lambda[l]), row(group_norm_g[l, 1]))

        y_c = _sb_attention(sb_qk, sb_vt, row(group_norm_g[l, 2]), batch)
        y_d = _hgrn2(hg_in, hgrn_lb_logits, row(hgrn_norm_g[l]), l, batch)

        h = _out_ffn(h, y_a, y_b, y_c, y_d, wout, row(ln2_g[l]), wg, wu, wd, row(final_norm_g),
                     l, final=(l == depth - 1))
    return h.reshape(batch, seq, d_model)
```

```python
import functools
import math

import numpy as np
import jax
import jax.numpy as jnp
from jax import lax
from jax.experimental import pallas as pl
from jax.experimental.pallas import tpu as pltpu

F32 = jnp.float32
BF16 = jnp.bfloat16

D_MODEL = 1024
N_HEADS = 4
MLA_Q_LORA = 256
MLA_KV_LORA = 128
MLA_NOPE = 64
MLA_ROPE = 32
MLA_V = 64
ROPE_THETA = 10000.0
LRU_WIDTH = 256
CONV_WIDTH = 4
LRU_C = 8.0
HEAD_DIM = 64
GROUP_W = 256
FFN_HIDDEN = 2816
NORM_EPS = 1e-6
MASK_VALUE = -1e30
LB_FLOOR = 1e-30

LANES = 128
SUBLANES = 8
VMEM_LIMIT_BYTES = 56 * 1024 * 1024

ROW_TILE = 512
ATTN_TILE = 256
SOFTMAX_ROWS = 64
SUM_ROWS = 16
PIPE_DEPTH = 2
ATTN_BODY_BLOCKS = 8
LOG2_E = math.log2(math.e)
LRU_TIME_TILE = 256
HG_CHUNK = 128
HG_LEVELS = int(math.log2(HG_CHUNK))
HG_SEQS = 4
FFN_CHUNK = 1408

IN_MLA = (0, 512)
IN_LRU = (512, 1024)
IN_SB = (1024, 1792)
IN_HG = (1792, 2816)


def _params(*semantics):
    return pltpu.CompilerParams(dimension_semantics=semantics, vmem_limit_bytes=VMEM_LIMIT_BYTES)


def _rms(x, g):
    return x * lax.rsqrt(jnp.mean(x * x, axis=-1, keepdims=True) + NORM_EPS) * g


def _dot(a, b):
    return jnp.dot(a, b, preferred_element_type=F32)


def _dot_nt(a, b):
    return lax.dot_general(a, b, (((1,), (1,)), ((), ())), preferred_element_type=F32)


def _dot_tn(a, b):
    return lax.dot_general(a, b, (((0,), (0,)), ((), ())), preferred_element_type=F32)


def _split_bf16(x):
    hi = x.astype(BF16)
    lo = (x - hi.astype(F32)).astype(BF16)
    return hi, lo


def _sigmoid(x):
    return 1.0 / (1.0 + jnp.exp(-x))


def _log1p_of_fraction(y):
    return jnp.log(1.0 + y)


def _log_sigmoid(x):
    return jnp.minimum(x, 0.0) - _log1p_of_fraction(jnp.exp(-jnp.abs(x)))


def _rope_table_kernel(pos_ref, freq_ref, m1_ref, m2_ref, c_ref, s1_ref, s2_ref):
    ang = pos_ref[...] * freq_ref[...]
    sn = jnp.sin(ang)
    c_ref[...] = jnp.cos(ang)
    s1_ref[...] = sn * m1_ref[...]
    s2_ref[...] = sn * m2_ref[...]


def _rope_tables(positions):
    m = positions.size
    half = MLA_ROPE // 2
    inv_freq = ROPE_THETA ** (-jnp.arange(half, dtype=F32) / half)
    zeros = jnp.zeros((MLA_NOPE,), F32)
    tail = jnp.zeros((LANES - MLA_NOPE - MLA_ROPE,), F32)
    freq = jnp.concatenate([zeros, inv_freq, inv_freq, tail]).reshape(1, LANES)
    ones = jnp.ones((half,), F32)
    m1 = jnp.concatenate([zeros, -ones, 0 * ones, tail]).reshape(1, LANES)
    m2 = jnp.concatenate([zeros, 0 * ones, ones, tail]).reshape(1, LANES)
    pos = positions.astype(F32).reshape(m, 1)
    tm = ROW_TILE
    row = pl.BlockSpec((1, LANES), lambda i: (0, 0))
    tab = pl.BlockSpec((tm, LANES), lambda i: (i, 0))
    shape = jax.ShapeDtypeStruct((m, LANES), F32)
    return pl.pallas_call(
        _rope_table_kernel,
        out_shape=(shape, shape, shape),
        grid=(m // tm,),
        in_specs=[pl.BlockSpec((tm, 1), lambda i: (i, 0)), row, row, row],
        out_specs=(tab, tab, tab),
        compiler_params=_params("arbitrary"),
        name="rope_tables",
    )(pos, freq, m1, m2)


def _in_proj_kernel(h_ref, g1_ref, win_ref, qg_ref, wuq_ref, kvg_ref, wukv_ref,
                    c_ref, s1_ref, s2_ref,
                    qm_ref, km_ref, vmt_ref, lrux_ref, lrug_ref, sb_ref, sbvt_ref, hg_ref):
    t = ATTN_TILE
    n_blocks = h_ref.shape[0] // t

    def store_transposed(ref, v):
        for blk in range(n_blocks):
            ref[blk] = v[blk * t:(blk + 1) * t, :].T.astype(BF16)

    u = _rms(h_ref[...], g1_ref[...]).astype(BF16)
    p = _dot(u, win_ref[:, IN_MLA[0]:IN_MLA[1]])
    lrux_ref[...] = _dot(u, win_ref[:, IN_LRU[0]:IN_LRU[0] + LRU_WIDTH])
    lrug_ref[...] = _dot(u, win_ref[:, IN_LRU[0] + LRU_WIDTH:IN_LRU[1]])
    cq = _rms(p[:, 0:MLA_Q_LORA], qg_ref[...]).astype(BF16)
    ckv = _rms(p[:, MLA_Q_LORA:MLA_Q_LORA + MLA_KV_LORA], kvg_ref[...]).astype(BF16)
    q = _dot(cq, wuq_ref[...])
    kv = _dot(ckv, wukv_ref[...])
    sb = _dot(u, win_ref[:, IN_SB[0]:IN_SB[1]])
    c, s1, s2 = c_ref[...], s1_ref[...], s2_ref[...]

    def rope(x):
        return x * c + pltpu.roll(x, LANES - MLA_ROPE // 2, 1) * s1 + pltpu.roll(x, MLA_ROPE // 2, 1) * s2

    kpe = rope(p[:, 384:512])
    for hh in range(N_HEADS):
        sl = slice(hh * LANES, (hh + 1) * LANES)
        qm_ref[:, sl] = rope(q[:, sl]).astype(BF16)
        km_ref[:, sl] = (kv[:, sl] + kpe).astype(BF16)
    store_transposed(vmt_ref, kv[:, N_HEADS * LANES:])
    hg_ref[...] = _dot(u, win_ref[:, IN_HG[0]:IN_HG[1]])
    sb_ref[...] = sb[:, :2 * GROUP_W].astype(BF16)
    store_transposed(sbvt_ref, sb[:, 2 * GROUP_W:])


def _in_proj(h, g1, win, qg, wuq, kvg, wukv, tabs, layer, batch):
    m = h.shape[0]
    seq = m // batch
    tm = ROW_TILE
    t = ATTN_TILE
    rows = lambda w: pl.BlockSpec((tm, w), lambda i: (i, 0))

    def full(a):
        if a.ndim == 3:
            return pl.BlockSpec((None,) + a.shape[1:], lambda i: (layer, 0, 0))
        return pl.BlockSpec(a.shape, lambda i: (0, 0))
    c, s1, s2 = tabs
    v_t = jax.ShapeDtypeStruct((m // t, GROUP_W, t), BF16)
    out_shape = (
        jax.ShapeDtypeStruct((m, N_HEADS * LANES), BF16),
        jax.ShapeDtypeStruct((m, N_HEADS * LANES), BF16),
        v_t,
        jax.ShapeDtypeStruct((seq, batch * LRU_WIDTH), F32),
        jax.ShapeDtypeStruct((seq, batch * LRU_WIDTH), F32),
        jax.ShapeDtypeStruct((m, 2 * GROUP_W), BF16),
        v_t,
        jax.ShapeDtypeStruct((m, 4 * GROUP_W), F32),
    )
    tiles_per_seq = seq // tm

    def out_spec(sd):
        if sd.shape[0] == seq:
            return pl.BlockSpec((tm, LRU_WIDTH), lambda i: (i % tiles_per_seq, i // tiles_per_seq))
        if len(sd.shape) == 3:
            return pl.BlockSpec((tm // t, GROUP_W, t), lambda i: (i, 0, 0))
        return rows(sd.shape[1])
    return pl.pallas_call(
        _in_proj_kernel,
        out_shape=out_shape,
        grid=(m // tm,),
        in_specs=[rows(D_MODEL), full(g1), full(win), full(qg), full(wuq), full(kvg), full(wukv),
                  rows(LANES), rows(LANES), rows(LANES)],
        out_specs=tuple(out_spec(sd) for sd in out_shape),
        compiler_params=_params("arbitrary"),
        name="in_proj",
    )(h, g1, win, qg, wuq, kvg, wukv, c, s1, s2)


def _pair_order(nq):
    dummy = (nq, 0)
    pad = lambda pairs: pairs + [dummy] * (-len(pairs) % ATTN_BODY_BLOCKS)
    diag = pad([(i, i) for i in range(nq)])
    off = pad([(i, i - d) for d in range(1, nq) for i in range(d, nq)])
    order = np.array([dummy] + diag + off + [dummy] * 2, np.int32)
    return (jnp.asarray(order[:, 0]), jnp.asarray(order[:, 1]),
            len(diag) // ATTN_BODY_BLOCKS, len(off) // ATTN_BODY_BLOCKS)


def _mla_kernel(qtab_ref, ktab_ref, q_ref, k_ref, vt_ref, gn_ref, o_ref,
                m_ref, acc_ref, s0_ref, s1_ref, p_ref, alpha_ref, *, diag_trips, off_trips):
    t = ATTN_TILE
    nq = q_ref.shape[0] // t
    m_ref[...] = jnp.full(m_ref.shape, MASK_VALUE, F32)
    acc_ref[...] = jnp.zeros(acc_ref.shape, F32)
    n_chunks = t // SOFTMAX_ROWS
    key_idx = lax.broadcasted_iota(jnp.int32, (SOFTMAX_ROWS, t), 0)
    qry_idx = lax.broadcasted_iota(jnp.int32, (SOFTMAX_ROWS, t), 1)
    ones_rows = jnp.ones((SUM_ROWS, t), BF16)
    s_refs = (s0_ref, s1_ref)

    def scores(n, hh, s_ref):
        q0 = pl.multiple_of(jnp.minimum(qtab_ref[n], nq - 1) * t, t)
        k0 = pl.multiple_of(ktab_ref[n] * t, t)
        sl = slice(hh * LANES, (hh + 1) * LANES)
        s_ref[hh] = _dot_nt(k_ref[pl.ds(k0, t), sl], q_ref[pl.ds(q0, t), sl])

    def softmax(n, hh, s_ref, diagonal):
        def chunk(c):
            s = s_ref[hh, c * SOFTMAX_ROWS:(c + 1) * SOFTMAX_ROWS, :]
            if diagonal:
                s = jnp.where(key_idx + c * SOFTMAX_ROWS <= qry_idx, s, MASK_VALUE)
            return s

        qb = qtab_ref[n]
        m_prev = m_ref[qb, hh]
        m_blk = chunk(0)
        for c in range(1, n_chunks):
            m_blk = jnp.maximum(m_blk, chunk(c))
        m_new = jnp.maximum(m_prev, jnp.max(m_blk, axis=0, keepdims=True))
        alpha_ref[hh] = jnp.exp2(m_prev - m_new)
        for c in range(n_chunks):
            p_ref[hh, c * SOFTMAX_ROWS:(c + 1) * SOFTMAX_ROWS, :] = jnp.exp2(chunk(c) - m_new).astype(BF16)
        m_ref[qb, hh] = m_new

    def weighted_values(n, hh):
        qb = qtab_ref[n]
        v_ext = jnp.concatenate([vt_ref[ktab_ref[n], hh * MLA_V:(hh + 1) * MLA_V, :], ones_rows], axis=0)
        acc_ref[qb, hh] = alpha_ref[hh] * acc_ref[qb, hh] + _dot(v_ext, p_ref[hh])

    def sweep(first, trips, diagonal):
        def body(r, carry):
            start = first + ATTN_BODY_BLOCKS * r
            previous = (jnp.where(r == 0, 0, start - 1), N_HEADS - 1)
            for i in range(ATTN_BODY_BLOCKS):
                for hh in range(N_HEADS):
                    scores(start + i + 1, hh, s_refs[(i + 1) % 2])
                    weighted_values(*previous)
                    softmax(start + i, hh, s_refs[i % 2], diagonal)
                    previous = (start + i, hh)
            return carry

        lax.fori_loop(0, trips, body, 0)
        weighted_values(first + ATTN_BODY_BLOCKS * trips - 1, N_HEADS - 1)

    alpha_ref[...] = jnp.zeros(alpha_ref.shape, F32)
    p_ref[...] = jnp.zeros(p_ref.shape, BF16)
    for hh in range(N_HEADS):
        scores(1, hh, s0_ref)
    sweep(1, diag_trips, True)
    sweep(1 + ATTN_BODY_BLOCKS * diag_trips, off_trips, False)

    def write_out(i, carry):
        r0 = pl.multiple_of(i * t, t)
        y_t = jnp.concatenate([acc_ref[i, hh, 0:MLA_V, :] / acc_ref[i, hh, MLA_V:MLA_V + 1, :]
                               for hh in range(N_HEADS)], axis=0)
        o_ref[pl.ds(r0, t), :] = _rms(y_t.T, gn_ref[...]).astype(BF16)
        return carry

    lax.fori_loop(0, nq, write_out, 0)


def _mla_attention(q, k, v_t, gn, batch):
    m = q.shape[0]
    s = m // batch
    t = ATTN_TILE
    nq = s // t
    qtab, ktab, diag_trips, off_trips = _pair_order(nq)
    smem = pl.BlockSpec(memory_space=pltpu.SMEM)
    return pl.pallas_call(
        functools.partial(_mla_kernel, diag_trips=diag_trips, off_trips=off_trips),
        out_shape=jax.ShapeDtypeStruct((m, GROUP_W), BF16),
        grid=(batch,),
        in_specs=[smem, smem,
                  pl.BlockSpec((s, N_HEADS * LANES), lambda b: (b, 0)),
                  pl.BlockSpec((s, N_HEADS * LANES), lambda b: (b, 0)),
                  pl.BlockSpec((nq, GROUP_W, t), lambda b: (b, 0, 0)),
                  pl.BlockSpec((1, GROUP_W), lambda b: (0, 0))],
        out_specs=pl.BlockSpec((s, GROUP_W), lambda b: (b, 0)),
        scratch_shapes=[pltpu.VMEM((nq + 1, N_HEADS, 1, t), F32),
                        pltpu.VMEM((nq + 1, N_HEADS, MLA_V + SUM_ROWS, t), F32),
                        pltpu.VMEM((N_HEADS, t, t), F32),
                        pltpu.VMEM((N_HEADS, t, t), F32),
                        pltpu.VMEM((N_HEADS, t, t), BF16),
                        pltpu.VMEM((N_HEADS, 1, t), F32)],
        compiler_params=_params("arbitrary"),
        name="mla_attention",
    )(qtab, ktab, q, k, v_t, gn)


def _sb_kernel(qtab_ref, ktab_ref, q_ref, k_ref, vt_ref, tri_ref, gn_ref, o_ref,
               qm_ref, carry_ref, acc_ref, z0_ref, z1_ref, lb_ref, lk_ref, later_ref, w_ref,
               *, diag_trips, off_trips):
    t = ATTN_TILE
    nq = q_ref.shape[0] // t
    n_chunks = t // SOFTMAX_ROWS
    lane = lax.broadcasted_iota(jnp.int32, (t, LANES), 1)

    def mask_queries(i, carry):
        r0 = pl.multiple_of(i * t, t)
        for hh in range(N_HEADS):
            qp = q_ref[pl.ds(r0, t), (hh // 2) * LANES:(hh // 2 + 1) * LANES]
            own = (lane >= (hh % 2) * HEAD_DIM) & (lane < (hh % 2 + 1) * HEAD_DIM)
            qm_ref[hh, pl.ds(r0, t), :] = jnp.where(own, qp, jnp.zeros_like(qp))
        return carry

    lax.fori_loop(0, nq, mask_queries, 0)
    qm_ref[:, nq * t:(nq + 1) * t, :] = jnp.zeros((N_HEADS, t, LANES), BF16)
    carry_ref[...] = jnp.zeros(carry_ref.shape, F32)
    acc_ref[...] = jnp.zeros(acc_ref.shape, F32)
    lb_ref[...] = jnp.zeros(lb_ref.shape, F32)
    later_ref[...] = jnp.zeros(later_ref.shape, F32)
    key_idx = lax.broadcasted_iota(jnp.int32, (SOFTMAX_ROWS, t), 0)
    qry_idx = lax.broadcasted_iota(jnp.int32, (SOFTMAX_ROWS, t), 1)

    def rows(c):
        return slice(c * SOFTMAX_ROWS, (c + 1) * SOFTMAX_ROWS)

    def logits(n, hh, z_ref):
        q0 = pl.multiple_of(qtab_ref[n] * t, t)
        k0 = pl.multiple_of(ktab_ref[n] * t, t)
        ps = slice((hh // 2) * LANES, (hh // 2 + 1) * LANES)
        z_ref[hh] = _dot_nt(k_ref[pl.ds(k0, t), ps], qm_ref[hh, pl.ds(q0, t), :])

    def log_terms(hh, z_ref, diagonal):
        for c in range(n_chunks):
            z = z_ref[hh, rows(c), :]
            softplus = jnp.log2(1.0 + jnp.exp2(-jnp.abs(z)))
            log_beta = jnp.minimum(z, 0.0) - softplus
            log_keep = log_beta - z
            if diagonal:
                log_keep = jnp.where(key_idx + c * SOFTMAX_ROWS < qry_idx, log_keep, 0.0)
            lb_ref[hh, rows(c), :] = log_beta
            lk_ref[hh, rows(c), :] = log_keep.astype(BF16)
        later_ref[hh] = _dot(tri_ref[...], lk_ref[hh])

    def weights_pv(n, hh, diagonal):
        qb = qtab_ref[n]
        carry = carry_ref[qb, hh]
        for c in range(n_chunks):
            w = jnp.exp2(lb_ref[hh, rows(c), :] + later_ref[hh, rows(c), :] + carry)
            if diagonal:
                w = jnp.where(key_idx + c * SOFTMAX_ROWS < qry_idx, w, 0.0)
            w_ref[hh, rows(c), :] = w.astype(BF16)
        carry_ref[qb, hh] = carry + later_ref[hh, t:t + 1, :]
        v_t = vt_ref[ktab_ref[n], hh * HEAD_DIM:(hh + 1) * HEAD_DIM, :]
        acc_ref[qb, hh] += _dot(v_t, w_ref[hh])

    tail_heads = range(N_HEADS - PIPE_DEPTH, N_HEADS)
    z_refs = (z0_ref, z1_ref)

    def trip(first, resumed, diagonal):
        pending = [(resumed, hh, diagonal) for hh in tail_heads]
        for i in range(ATTN_BODY_BLOCKS):
            for hh in range(N_HEADS):
                ahead = hh + PIPE_DEPTH
                if ahead < N_HEADS:
                    logits(first + i, ahead, z_refs[i % 2])
                else:
                    logits(first + i + 1, ahead - N_HEADS, z_refs[(i + 1) % 2])
                weights_pv(*pending.pop(0))
                log_terms(hh, z_refs[i % 2], diagonal)
                pending.append((first + i, hh, diagonal))

    def sweep(first, trips, diagonal):
        def body(r, carry):
            start = first + ATTN_BODY_BLOCKS * r
            trip(start, jnp.where(r == 0, 0, start - 1), diagonal)
            return carry

        lax.fori_loop(0, trips, body, 0)
        last = first + ATTN_BODY_BLOCKS * trips - 1
        for hh in tail_heads:
            weights_pv(last, hh, diagonal)

    for hh in range(PIPE_DEPTH):
        logits(1, hh, z0_ref)
    sweep(1, diag_trips, True)
    sweep(1 + ATTN_BODY_BLOCKS * diag_trips, off_trips, False)

    def write_out(i, carry):
        r0 = pl.multiple_of(i * t, t)
        y_t = jnp.concatenate([acc_ref[i, hh] for hh in range(N_HEADS)], axis=0)
        o_ref[pl.ds(r0, t), :] = _rms(y_t.T, gn_ref[...]).astype(BF16)
        return carry

    lax.fori_loop(0, nq, write_out, 0)


def _sb_attention(qk, v_t, gn, batch):
    m = qk.shape[0]
    s = m // batch
    t = ATTN_TILE
    nq = s // t
    idx = np.arange(t)
    later = idx[None, :] > idx[:, None]
    total = np.ones((SUM_ROWS, t), bool)
    tri = jnp.asarray(np.concatenate([later, total], axis=0), BF16)
    qtab, ktab, diag_trips, off_trips = _pair_order(nq)
    smem = pl.BlockSpec(memory_space=pltpu.SMEM)
    return pl.pallas_call(
        functools.partial(_sb_kernel, diag_trips=diag_trips, off_trips=off_trips),
        out_shape=jax.ShapeDtypeStruct((m, GROUP_W), BF16),
        grid=(batch,),
        in_specs=[smem, smem,
                  pl.BlockSpec((s, GROUP_W), lambda b: (b, 0)),
                  pl.BlockSpec((s, GROUP_W), lambda b: (b, 1)),
                  pl.BlockSpec((nq, GROUP_W, t), lambda b: (b, 0, 0)),
                  pl.BlockSpec(tri.shape, lambda b: (0, 0)),
                  pl.BlockSpec((1, GROUP_W), lambda b: (0, 0))],
        out_specs=pl.BlockSpec((s, GROUP_W), lambda b: (b, 0)),
        scratch_shapes=[pltpu.VMEM((N_HEADS, (nq + 1) * t, LANES), BF16),
                        pltpu.VMEM((nq + 1, N_HEADS, 1, t), F32),
                        pltpu.VMEM((nq + 1, N_HEADS, HEAD_DIM, t), F32),
                        pltpu.VMEM((N_HEADS, t, t), F32),
                        pltpu.VMEM((N_HEADS, t, t), F32),
                        pltpu.VMEM((N_HEADS, t, t), F32),
                        pltpu.VMEM((N_HEADS, t, t), BF16),
                        pltpu.VMEM((N_HEADS, t + SUM_ROWS, t), F32),
                        pltpu.VMEM((N_HEADS, t, t), BF16)],
        compiler_params=_params("arbitrary"),
        name="sb_attention",
    )(qtab, ktab, qk, qk, v_t, tri, gn)


def _lru_kernel(x_ref, gate_ref, cw_ref, cb_ref, wab_ref, ba_ref, bx_ref, lam_ref, gn_ref,
                o_ref, xs_ref, a_ref, u_ref, h_ref):
    ts, wide = x_ref.shape
    w = LRU_WIDTH
    seqs = range(wide // w)
    pad = CONV_WIDTH - 1
    hist = SUBLANES

    @pl.when(pl.program_id(0) == 0)
    def _():
        xs_ref[0:hist] = jnp.zeros((hist, wide), F32)
        h_ref[...] = jnp.zeros(h_ref.shape, F32)

    @pl.when(pl.program_id(0) > 0)
    def _():
        xs_ref[0:hist] = xs_ref[ts:ts + hist]

    xs_ref[hist:hist + ts] = x_ref[...]
    lam = lam_ref[...]
    softplus_neg_lam = jnp.maximum(-lam, 0.0) + jnp.log1p(jnp.exp(-jnp.abs(lam)))
    for b in seqs:
        sl = slice(b * w, (b + 1) * w)
        xc = cb_ref[...]
        for i in range(CONV_WIDTH):
            xc = xc + cw_ref[i:i + 1, :] * xs_ref[hist - pad + i:hist - pad + i + ts, sl]
        gates = _dot(xc.astype(BF16), wab_ref[...])
        rec_gate = _sigmoid(gates[:, :w] + ba_ref[...])
        in_gate = _sigmoid(gates[:, w:] + bx_ref[...])
        log_a = -LRU_C * rec_gate * softplus_neg_lam
        a = jnp.exp(log_a)
        a_ref[:, sl] = a
        one_minus_a2 = -jnp.tanh(log_a) * (a * a + 1.0)
        u_ref[:, sl] = jnp.sqrt(jnp.maximum(one_minus_a2, 0.0)) * (in_gate * xc)

    def step(i, h):
        h = a_ref[pl.ds(i, 1), :] * h + u_ref[pl.ds(i, 1), :]
        u_ref[pl.ds(i, 1), :] = h
        return h

    h_ref[...] = lax.fori_loop(0, ts, step, h_ref[...], unroll=8)

    for b in seqs:
        sl = slice(b * w, (b + 1) * w)
        g = gate_ref[:, sl]
        gelu = 0.5 * g * (1.0 + jnp.tanh(math.sqrt(2.0 / math.pi) * (g + 0.044715 * (g * g * g))))
        o_ref[:, sl] = _rms(u_ref[:, sl] * gelu, gn_ref[...])


def _rglru(x_t, gate_t, cw, cb, wab, ba, bx, lam, gn):
    s, wide = x_t.shape
    ts = LRU_TIME_TILE
    tile = pl.BlockSpec((ts, wide), lambda i: (i, 0))
    full = lambda a: pl.BlockSpec(a.shape, lambda i: (0,) * a.ndim)
    return pl.pallas_call(
        _lru_kernel,
        out_shape=jax.ShapeDtypeStruct((s, wide), F32),
        grid=(s // ts,),
        in_specs=[tile, tile, full(cw), full(cb), full(wab), full(ba), full(bx), full(lam), full(gn)],
        out_specs=tile,
        scratch_shapes=[pltpu.VMEM((ts + SUBLANES, wide), F32),
                        pltpu.VMEM((ts, wide), F32),
                        pltpu.VMEM((ts, wide), F32),
                        pltpu.VMEM((1, wide), F32)],
        compiler_params=_params("arbitrary"),
        name="rglru",
    )(x_t, gate_t, cw, cb, wab, ba, bx, lam, gn)


def _hg_constants():
    t = HG_CHUNK
    idx = np.arange(t)
    sums = [idx[:, None] >= idx[None, :]]
    masks = [idx[:, None] == idx[None, :]]
    for lev in range(HG_LEVELS):
        half = 1 << lev
        mid = (idx // (2 * half)) * (2 * half) + half
        upper = idx >= mid
        j = idx[None, :]
        up_rows = upper[:, None] & (j >= mid[:, None]) & (j <= idx[:, None])
        low_rows = (~upper)[:, None] & (j > idx[:, None]) & (j < mid[:, None])
        sums.append(up_rows | low_rows)
        same = (idx[:, None] // (2 * half)) == (idx[None, :] // (2 * half))
        masks.append(same & upper[:, None] & (~upper)[None, :])
    sum_mat = np.concatenate(sums, axis=0)
    sum_mat = jnp.asarray(np.concatenate([sum_mat, sum_mat], axis=1), BF16)
    mask = jnp.asarray(np.stack(masks, axis=0), F32)
    head = np.arange(GROUP_W) // HEAD_DIM
    same_head = head[:, None] == head[None, :]
    same_head = jnp.asarray(np.concatenate([same_head, same_head], axis=0), BF16)
    return sum_mat, mask, same_head


def _hg_kernel(x_ref, lbl_ref, ng_ref, sum_ref, mask_ref, same_ref, o_ref, st_ref, *, layer):
    t = HG_CHUNK
    w = GROUP_W

    seqs = range(x_ref.shape[0])

    @pl.when(pl.program_id(1) == 0)
    def _():
        st_ref[...] = jnp.zeros(st_ref.shape, F32)

    logits = lbl_ref[...]
    ex = jnp.exp(logits - jnp.max(logits, axis=0, keepdims=True))
    prob = ex / jnp.sum(ex, axis=0, keepdims=True)
    csum = prob[0:1, :]
    for i in range(1, layer + 1):
        csum = csum + prob[i:i + 1, :]
    lb = csum - prob[0:1, :]
    a_ = jnp.log(jnp.maximum(lb, LB_FLOOR))
    log_one_minus_lb = jnp.log1p(-lb)
    same = same_ref[...]
    lane_head = lax.broadcasted_iota(jnp.int32, (t, w), 1) // HEAD_DIM

    def by_head(x):
        return jnp.concatenate([jnp.where(lane_head == hh, x, jnp.zeros_like(x)) for hh in range(N_HEADS)], axis=0)

    q = [x_ref[b, :, 0:w] for b in seqs]
    vb = [x_ref[b, :, 2 * w:3 * w].astype(BF16) for b in seqs]
    key, sums = [], []
    for b in seqs:
        fz = x_ref[b, :, w:2 * w]
        b_ = log_one_minus_lb + _log_sigmoid(fz)
        log_f = jnp.maximum(a_, b_) + _log1p_of_fraction(jnp.exp(-jnp.abs(a_ - b_)))
        key.append((1.0 - lb) * _sigmoid(-fz))
        sums.append(_dot(sum_ref[...], jnp.concatenate(_split_bf16(log_f), axis=0)))

    out = []
    for b in seqs:
        cum = sums[b][0:t]
        last = cum[t - 1:t, :]
        st = st_ref[b]
        out.append(_dot_nt((q[b] * jnp.exp(cum)).astype(BF16), st.astype(BF16)))
        kdec = (key[b] * jnp.exp(last - cum)).astype(BF16)
        st_ref[b] = st * jnp.exp(last) + jnp.where(same[0:w] > 0, _dot_tn(vb[b], kdec), 0.0)

    scores = [[jnp.zeros((t, t), F32) for _ in range(N_HEADS)] for _ in seqs]
    for lev in range(HG_LEVELS + 1):
        for b in seqs:
            if lev == 0:
                qt, kt = q[b], key[b]
            else:
                decay = jnp.exp(sums[b][lev * t:(lev + 1) * t])
                qt, kt = q[b] * decay, key[b] * decay
            sc = _dot_nt(qt.astype(BF16), by_head(kt.astype(BF16)))
            for hh in range(N_HEADS):
                scores[b][hh] = scores[b][hh] + sc[:, hh * t:(hh + 1) * t] * mask_ref[lev]
    for b in seqs:
        all_scores = jnp.concatenate([s.astype(BF16) for s in scores[b]], axis=1)
        out[b] = out[b] + _dot(all_scores, by_head(vb[b]))

    mean_sq = [_dot(jnp.concatenate(_split_bf16(out[b] * out[b]), axis=1), same) * (1.0 / HEAD_DIM) for b in seqs]
    for b in seqs:
        g = x_ref[b, :, 3 * w:4 * w]
        normed = out[b] * lax.rsqrt(mean_sq[b] + NORM_EPS) * ng_ref[...]
        o_ref[b] = (normed * (g * _sigmoid(g))).astype(BF16)


def _hgrn2(x, lb_logits, ng, layer, batch):
    m = x.shape[0]
    t = HG_CHUNK
    nb = HG_SEQS
    seq = m // batch
    sum_mat, mask, same_head = _hg_constants()
    full = lambda a: pl.BlockSpec(a.shape, lambda b, c: (0,) * a.ndim)
    y = pl.pallas_call(
        functools.partial(_hg_kernel, layer=layer),
        out_shape=jax.ShapeDtypeStruct((batch, seq, GROUP_W), BF16),
        grid=(batch // nb, seq // t),
        in_specs=[pl.BlockSpec((nb, t, 4 * GROUP_W), lambda b, c: (b, c, 0)),
                  full(lb_logits), full(ng), full(sum_mat), full(mask), full(same_head)],
        out_specs=pl.BlockSpec((nb, t, GROUP_W), lambda b, c: (b, c, 0)),
        scratch_shapes=[pltpu.VMEM((nb, GROUP_W, GROUP_W), F32)],
        compiler_params=_params("arbitrary", "arbitrary"),
        name="hgrn2",
    )(x.reshape(batch, seq, 4 * GROUP_W), lb_logits, ng, sum_mat, mask, same_head)
    return y.reshape(m, GROUP_W)


def _out_ffn_kernel(h_ref, ya_ref, yb_ref, yc_ref, yd_ref, wout_ref, g2_ref, wg_ref, wu_ref, wd_ref,
                    gf_ref, o_ref, *, final):
    mixed = jnp.concatenate([ya_ref[...], yb_ref[...].astype(BF16), yc_ref[...], yd_ref[...]], axis=1)
    h1 = h_ref[...] + _dot(mixed, wout_ref[...])
    un = _rms(h1, g2_ref[...]).astype(BF16)
    chunks = [slice(c * FFN_CHUNK, (c + 1) * FFN_CHUNK) for c in range(FFN_HIDDEN // FFN_CHUNK)]
    gate_up = [(_dot(un, wg_ref[:, sl]), _dot(un, wu_ref[:, sl])) for sl in chunks]
    acc = h1
    for sl, (gt, up) in zip(chunks, gate_up):
        act = (gt * _sigmoid(gt) * up).astype(BF16)
        acc = acc + _dot(act, wd_ref[sl, :])
    if final:
        acc = _rms(acc, gf_ref[...])
    o_ref[...] = acc


def _out_ffn(h, ya, yb, yc, yd, wout, g2, wg, wu, wd, gf, layer, final):
    m = h.shape[0]
    tm = ROW_TILE
    rows = lambda w: pl.BlockSpec((tm, w), lambda i: (i, 0))

    def once(a):
        if a.ndim == 3:
            return pl.BlockSpec((None,) + a.shape[1:], lambda i: (layer, 0, 0), pipeline_mode=pl.Buffered(1))
        return pl.BlockSpec(a.shape, lambda i: (0, 0), pipeline_mode=pl.Buffered(1))
    tiles_per_seq = yb.shape[0] // tm
    yb_rows = pl.BlockSpec((tm, GROUP_W), lambda i: (i % tiles_per_seq, i // tiles_per_seq))
    return pl.pallas_call(
        functools.partial(_out_ffn_kernel, final=final),
        out_shape=jax.ShapeDtypeStruct((m, D_MODEL), F32),
        grid=(m // tm,),
        in_specs=[rows(D_MODEL), rows(GROUP_W), yb_rows, rows(GROUP_W), rows(GROUP_W),
                  once(wout), once(g2), once(wg), once(wu), once(wd), once(gf)],
        out_specs=rows(D_MODEL),
        compiler_params=_params("arbitrary"),
        name="out_ffn",
    )(h, ya, yb, yc, yd, wout, g2, wg, wu, wd, gf)


def _pad_in_proj(w):
    z = lambda n: jnp.zeros(w.shape[:-1] + (n,), w.dtype)
    sb_q = w[..., 928:928 + GROUP_W] * (LOG2_E * HEAD_DIM ** -0.5)
    return jnp.concatenate([w[..., :384], z(MLA_NOPE), w[..., 384:416], z(LANES - MLA_NOPE - MLA_ROPE),
                            w[..., 416:928], sb_q, w[..., 928 + GROUP_W:]], axis=-1).astype(BF16)


def _pad_heads(w, width):
    lead = w.shape[:-1]
    w = w.reshape(lead + (N_HEADS, width))
    w = jnp.pad(w, ((0, 0),) * (len(lead) + 1) + ((0, LANES - width),))
    return w.reshape(lead + (N_HEADS * LANES,))


def _block_diag(w):
    n, c, d = w.shape[-3:]
    eye = jnp.eye(n, dtype=w.dtype)
    return (eye[:, None, :, None] * w[..., :, :, None, :]).reshape(w.shape[:-3] + (n * c, n * d))


def kernel(x, positions, ln1_g, w_in, mla_q_norm_g, mla_w_uq, mla_kv_norm_g, mla_w_ukv, lru_conv_w, lru_conv_b, lru_w_a, lru_b_a, lru_w_x, lru_b_x, lru_lambda, hgrn_lb_logits, hgrn_norm_g, group_norm_g, w_out, ln2_g, w_ffn_gate, w_ffn_up, w_ffn_down, final_norm_g):
    batch, seq, d_model = x.shape
    depth = w_in.shape[0]
    m = batch * seq
    row = lambda v: v.reshape(1, -1)
    h = x.reshape(m, d_model)
    tabs = _rope_tables(positions)
    qk_scale = LOG2_E * (MLA_NOPE + MLA_ROPE) ** -0.5

    win = _pad_in_proj(w_in)
    wuq = _pad_heads(mla_w_uq * qk_scale, MLA_NOPE + MLA_ROPE).astype(BF16)
    wukv = mla_w_ukv.reshape(depth, MLA_KV_LORA, N_HEADS, MLA_NOPE + MLA_V)
    wuk = _pad_heads(wukv[..., :MLA_NOPE].reshape(depth, MLA_KV_LORA, -1), MLA_NOPE)
    wuv = wukv[..., MLA_NOPE:].reshape(depth, MLA_KV_LORA, -1)
    wukv = jnp.concatenate([wuk, wuv], axis=-1).astype(BF16)
    wab = jnp.concatenate([_block_diag(lru_w_a), _block_diag(lru_w_x)], axis=-1).astype(BF16)
    wout, wg, wu, wd = (w.astype(BF16) for w in (w_out, w_ffn_gate, w_ffn_up, w_ffn_down))

    for l in range(depth):
        qm, km, vm_t, lru_x, lru_gate, sb_qk, sb_vt, hg_in = _in_proj(
            h, row(ln1_g[l]), win, row(mla_q_norm_g[l]), wuq, row(mla_kv_norm_g[l]), wukv, tabs, l, batch)

        y_a = _mla_attention(qm, km, vm_t, row(group_norm_g[l, 0]), batch)
        y_b = _rglru(lru_x, lru_gate, lru_conv_w[l], row(lru_conv_b[l]), wab[l], row(lru_b_a[l]),
                     row(lru_b_x[l]), row(lru_lambda[l]), row(group_norm_g[l, 1]))

        y_c = _sb_attention(sb_qk, sb_vt, row(group_norm_g[l, 2]), batch)
        y_d = _hgrn2(hg_in, hgrn_lb_logits, row(hgrn_norm_g[l]), l, batch)

        h = _out_ffn(h, y_a, y_b, y_c, y_d, wout, row(ln2_g[l]), wg, wu, wd, row(final_norm_g),
                     l, final=(l == depth - 1))
    return h.reshape(batch, seq, d_model)
```

```python
import functools
import math

import numpy as np
import jax
import jax.numpy as jnp
from jax import lax
from jax.experimental import pallas as pl
from jax.experimental.pallas import tpu as pltpu

F32 = jnp.float32
BF16 = jnp.bfloat16

D_MODEL = 1024
N_HEADS = 4
MLA_Q_LORA = 256
MLA_KV_LORA = 128
MLA_NOPE = 64
MLA_ROPE = 32
MLA_V = 64
ROPE_THETA = 10000.0
LRU_WIDTH = 256
CONV_WIDTH = 4
LRU_C = 8.0
HEAD_DIM = 64
GROUP_W = 256
FFN_HIDDEN = 2816
NORM_EPS = 1e-6
MASK_VALUE = -1e30
LB_FLOOR = 1e-30

LANES = 128
SUBLANES = 8
VMEM_LIMIT_BYTES = 56 * 1024 * 1024

ROW_TILE = 512
ATTN_TILE = 256
SOFTMAX_ROWS = 64
SUM_ROWS = 16
PIPE_DEPTH = 2
ATTN_BODY_BLOCKS = 8
LOG2_E = math.log2(math.e)
LRU_TIME_TILE = 256
HG_CHUNK = 128
HG_LEVELS = int(math.log2(HG_CHUNK))
HG_SEQS = 4
FFN_CHUNK = FFN_HIDDEN

IN_MLA = (0, 512)
IN_LRU = (512, 1024)
IN_SB = (1024, 1792)
IN_HG = (1792, 2816)


def _params(*semantics):
    return pltpu.CompilerParams(dimension_semantics=semantics, vmem_limit_bytes=VMEM_LIMIT_BYTES)


def _rms(x, g):
    return x * lax.rsqrt(jnp.mean(x * x, axis=-1, keepdims=True) + NORM_EPS) * g


def _dot(a, b):
    return jnp.dot(a, b, preferred_element_type=F32)


def _dot_nt(a, b):
    return lax.dot_general(a, b, (((1,), (1,)), ((), ())), preferred_element_type=F32)


def _dot_tn(a, b):
    return lax.dot_general(a, b, (((0,), (0,)), ((), ())), preferred_element_type=F32)


def _split_bf16(x):
    hi = x.astype(BF16)
    lo = (x - hi.astype(F32)).astype(BF16)
    return hi, lo


def _sigmoid(x):
    return 1.0 / (1.0 + jnp.exp(-x))


def _log1p_of_fraction(y):
    return jnp.log(1.0 + y)


def _log_sigmoid(x):
    return jnp.minimum(x, 0.0) - _log1p_of_fraction(jnp.exp(-jnp.abs(x)))


def _rope_table_kernel(pos_ref, freq_ref, m1_ref, m2_ref, c_ref, s1_ref, s2_ref):
    ang = pos_ref[...] * freq_ref[...]
    sn = jnp.sin(ang)
    c_ref[...] = jnp.cos(ang)
    s1_ref[...] = sn * m1_ref[...]
    s2_ref[...] = sn * m2_ref[...]


def _rope_tables(positions):
    m = positions.size
    half = MLA_ROPE // 2
    inv_freq = ROPE_THETA ** (-jnp.arange(half, dtype=F32) / half)
    zeros = jnp.zeros((MLA_NOPE,), F32)
    tail = jnp.zeros((LANES - MLA_NOPE - MLA_ROPE,), F32)
    freq = jnp.concatenate([zeros, inv_freq, inv_freq, tail]).reshape(1, LANES)
    ones = jnp.ones((half,), F32)
    m1 = jnp.concatenate([zeros, -ones, 0 * ones, tail]).reshape(1, LANES)
    m2 = jnp.concatenate([zeros, 0 * ones, ones, tail]).reshape(1, LANES)
    pos = positions.astype(F32).reshape(m, 1)
    tm = ROW_TILE
    row = pl.BlockSpec((1, LANES), lambda i: (0, 0))
    tab = pl.BlockSpec((tm, LANES), lambda i: (i, 0))
    shape = jax.ShapeDtypeStruct((m, LANES), F32)
    return pl.pallas_call(
        _rope_table_kernel,
        out_shape=(shape, shape, shape),
        grid=(m // tm,),
        in_specs=[pl.BlockSpec((tm, 1), lambda i: (i, 0)), row, row, row],
        out_specs=(tab, tab, tab),
        compiler_params=_params("arbitrary"),
        name="rope_tables",
    )(pos, freq, m1, m2)


def _in_proj_kernel(h_ref, g1_ref, win_ref, qg_ref, wuq_ref, kvg_ref, wukv_ref,
                    c_ref, s1_ref, s2_ref,
                    qm_ref, km_ref, vmt_ref, lrux_ref, lrug_ref, sb_ref, sbvt_ref, hg_ref):
    t = ATTN_TILE
    n_blocks = h_ref.shape[0] // t

    def store_transposed(ref, v):
        for blk in range(n_blocks):
            ref[blk] = v[blk * t:(blk + 1) * t, :].T.astype(BF16)

    u = _rms(h_ref[...], g1_ref[...]).astype(BF16)
    p = _dot(u, win_ref[:, IN_MLA[0]:IN_MLA[1]])
    lrux_ref[...] = _dot(u, win_ref[:, IN_LRU[0]:IN_LRU[0] + LRU_WIDTH])
    lrug_ref[...] = _dot(u, win_ref[:, IN_LRU[0] + LRU_WIDTH:IN_LRU[1]])
    cq = _rms(p[:, 0:MLA_Q_LORA], qg_ref[...]).astype(BF16)
    ckv = _rms(p[:, MLA_Q_LORA:MLA_Q_LORA + MLA_KV_LORA], kvg_ref[...]).astype(BF16)
    q = _dot(cq, wuq_ref[...])
    kv = _dot(ckv, wukv_ref[...])
    sb = _dot(u, win_ref[:, IN_SB[0]:IN_SB[1]])
    c, s1, s2 = c_ref[...], s1_ref[...], s2_ref[...]

    def rope(x):
        return x * c + pltpu.roll(x, LANES - MLA_ROPE // 2, 1) * s1 + pltpu.roll(x, MLA_ROPE // 2, 1) * s2

    kpe = rope(p[:, 384:512])
    for hh in range(N_HEADS):
        sl = slice(hh * LANES, (hh + 1) * LANES)
        qm_ref[:, sl] = rope(q[:, sl]).astype(BF16)
        km_ref[:, sl] = (kv[:, sl] + kpe).astype(BF16)
    store_transposed(vmt_ref, kv[:, N_HEADS * LANES:])
    hg_ref[...] = _dot(u, win_ref[:, IN_HG[0]:IN_HG[1]])
    sb_ref[...] = sb[:, :2 * GROUP_W].astype(BF16)
    store_transposed(sbvt_ref, sb[:, 2 * GROUP_W:])


def _in_proj(h, g1, win, qg, wuq, kvg, wukv, tabs, layer, batch):
    m = h.shape[0]
    seq = m // batch
    tm = ROW_TILE
    t = ATTN_TILE
    rows = lambda w: pl.BlockSpec((tm, w), lambda i: (i, 0))

    def full(a):
        if a.ndim == 3:
            return pl.BlockSpec((None,) + a.shape[1:], lambda i: (layer, 0, 0))
        return pl.BlockSpec(a.shape, lambda i: (0, 0))
    c, s1, s2 = tabs
    v_t = jax.ShapeDtypeStruct((m // t, GROUP_W, t), BF16)
    out_shape = (
        jax.ShapeDtypeStruct((m, N_HEADS * LANES), BF16),
        jax.ShapeDtypeStruct((m, N_HEADS * LANES), BF16),
        v_t,
        jax.ShapeDtypeStruct((seq, batch * LRU_WIDTH), F32),
        jax.ShapeDtypeStruct((seq, batch * LRU_WIDTH), F32),
        jax.ShapeDtypeStruct((m, 2 * GROUP_W), BF16),
        v_t,
        jax.ShapeDtypeStruct((m, 4 * GROUP_W), F32),
    )
    tiles_per_seq = seq // tm

    def out_spec(sd):
        if sd.shape[0] == seq:
            return pl.BlockSpec((tm, LRU_WIDTH), lambda i: (i % tiles_per_seq, i // tiles_per_seq))
        if len(sd.shape) == 3:
            return pl.BlockSpec((tm // t, GROUP_W, t), lambda i: (i, 0, 0))
        return rows(sd.shape[1])
    return pl.pallas_call(
        _in_proj_kernel,
        out_shape=out_shape,
        grid=(m // tm,),
        in_specs=[rows(D_MODEL), full(g1), full(win), full(qg), full(wuq), full(kvg), full(wukv),
                  rows(LANES), rows(LANES), rows(LANES)],
        out_specs=tuple(out_spec(sd) for sd in out_shape),
        compiler_params=_params("arbitrary"),
        name="in_proj",
    )(h, g1, win, qg, wuq, kvg, wukv, c, s1, s2)


def _pair_order(nq):
    dummy = (nq, 0)
    pad = lambda pairs: pairs + [dummy] * (-len(pairs) % ATTN_BODY_BLOCKS)
    diag = pad([(i, i) for i in range(nq)])
    off = pad([(i, i - d) for d in range(1, nq) for i in range(d, nq)])
    order = np.array([dummy] + diag + off + [dummy] * 2, np.int32)
    return (jnp.asarray(order[:, 0]), jnp.asarray(order[:, 1]),
            len(diag) // ATTN_BODY_BLOCKS, len(off) // ATTN_BODY_BLOCKS)


def _mla_kernel(qtab_ref, ktab_ref, q_ref, k_ref, vt_ref, gn_ref, o_ref,
                m_ref, acc_ref, s0_ref, s1_ref, p_ref, alpha_ref, *, diag_trips, off_trips):
    t = ATTN_TILE
    nq = q_ref.shape[0] // t
    m_ref[...] = jnp.full(m_ref.shape, MASK_VALUE, F32)
    acc_ref[...] = jnp.zeros(acc_ref.shape, F32)
    n_chunks = t // SOFTMAX_ROWS
    key_idx = lax.broadcasted_iota(jnp.int32, (SOFTMAX_ROWS, t), 0)
    qry_idx = lax.broadcasted_iota(jnp.int32, (SOFTMAX_ROWS, t), 1)
    ones_rows = jnp.ones((SUM_ROWS, t), BF16)
    s_refs = (s0_ref, s1_ref)

    def scores(n, hh, s_ref):
        q0 = pl.multiple_of(jnp.minimum(qtab_ref[n], nq - 1) * t, t)
        k0 = pl.multiple_of(ktab_ref[n] * t, t)
        sl = slice(hh * LANES, (hh + 1) * LANES)
        s_ref[hh] = _dot_nt(k_ref[pl.ds(k0, t), sl], q_ref[pl.ds(q0, t), sl])

    def softmax(n, hh, s_ref, diagonal):
        def chunk(c):
            s = s_ref[hh, c * SOFTMAX_ROWS:(c + 1) * SOFTMAX_ROWS, :]
            if diagonal:
                s = jnp.where(key_idx + c * SOFTMAX_ROWS <= qry_idx, s, MASK_VALUE)
            return s

        qb = qtab_ref[n]
        m_prev = m_ref[qb, hh]
        m_blk = chunk(0)
        for c in range(1, n_chunks):
            m_blk = jnp.maximum(m_blk, chunk(c))
        m_new = jnp.maximum(m_prev, jnp.max(m_blk, axis=0, keepdims=True))
        alpha_ref[hh] = jnp.exp2(m_prev - m_new)
        for c in range(n_chunks):
            p_ref[hh, c * SOFTMAX_ROWS:(c + 1) * SOFTMAX_ROWS, :] = jnp.exp2(chunk(c) - m_new).astype(BF16)
        m_ref[qb, hh] = m_new

    def weighted_values(n, hh):
        qb = qtab_ref[n]
        v_ext = jnp.concatenate([vt_ref[ktab_ref[n], hh * MLA_V:(hh + 1) * MLA_V, :], ones_rows], axis=0)
        acc_ref[qb, hh] = alpha_ref[hh] * acc_ref[qb, hh] + _dot(v_ext, p_ref[hh])

    def sweep(first, trips, diagonal):
        def body(r, carry):
            start = first + ATTN_BODY_BLOCKS * r
            previous = (jnp.where(r == 0, 0, start - 1), N_HEADS - 1)
            for i in range(ATTN_BODY_BLOCKS):
                for hh in range(N_HEADS):
                    scores(start + i + 1, hh, s_refs[(i + 1) % 2])
                    weighted_values(*previous)
                    softmax(start + i, hh, s_refs[i % 2], diagonal)
                    previous = (start + i, hh)
            return carry

        lax.fori_loop(0, trips, body, 0)
        weighted_values(first + ATTN_BODY_BLOCKS * trips - 1, N_HEADS - 1)

    alpha_ref[...] = jnp.zeros(alpha_ref.shape, F32)
    p_ref[...] = jnp.zeros(p_ref.shape, BF16)
    for hh in range(N_HEADS):
        scores(1, hh, s0_ref)
    sweep(1, diag_trips, True)
    sweep(1 + ATTN_BODY_BLOCKS * diag_trips, off_trips, False)

    def write_out(i, carry):
        r0 = pl.multiple_of(i * t, t)
        y_t = jnp.concatenate([acc_ref[i, hh, 0:MLA_V, :] / acc_ref[i, hh, MLA_V:MLA_V + 1, :]
                               for hh in range(N_HEADS)], axis=0)
        o_ref[pl.ds(r0, t), :] = _rms(y_t.T, gn_ref[...]).astype(BF16)
        return carry

    lax.fori_loop(0, nq, write_out, 0)


def _mla_attention(q, k, v_t, gn, batch):
    m = q.shape[0]
    s = m // batch
    t = ATTN_TILE
    nq = s // t
    qtab, ktab, diag_trips, off_trips = _pair_order(nq)
    smem = pl.BlockSpec(memory_space=pltpu.SMEM)
    return pl.pallas_call(
        functools.partial(_mla_kernel, diag_trips=diag_trips, off_trips=off_trips),
        out_shape=jax.ShapeDtypeStruct((m, GROUP_W), BF16),
        grid=(batch,),
        in_specs=[smem, smem,
                  pl.BlockSpec((s, N_HEADS * LANES), lambda b: (b, 0)),
                  pl.BlockSpec((s, N_HEADS * LANES), lambda b: (b, 0)),
                  pl.BlockSpec((nq, GROUP_W, t), lambda b: (b, 0, 0)),
                  pl.BlockSpec((1, GROUP_W), lambda b: (0, 0))],
        out_specs=pl.BlockSpec((s, GROUP_W), lambda b: (b, 0)),
        scratch_shapes=[pltpu.VMEM((nq + 1, N_HEADS, 1, t), F32),
                        pltpu.VMEM((nq + 1, N_HEADS, MLA_V + SUM_ROWS, t), F32),
                        pltpu.VMEM((N_HEADS, t, t), F32),
                        pltpu.VMEM((N_HEADS, t, t), F32),
                        pltpu.VMEM((N_HEADS, t, t), BF16),
                        pltpu.VMEM((N_HEADS, 1, t), F32)],
        compiler_params=_params("arbitrary"),
        name="mla_attention",
    )(qtab, ktab, q, k, v_t, gn)


def _sb_kernel(qtab_ref, ktab_ref, q_ref, k_ref, vt_ref, tri_ref, gn_ref, o_ref,
               qm_ref, carry_ref, acc_ref, z0_ref, z1_ref, lb_ref, lk_ref, later_ref, w_ref,
               *, diag_trips, off_trips):
    t = ATTN_TILE
    nq = q_ref.shape[0] // t
    n_chunks = t // SOFTMAX_ROWS
    lane = lax.broadcasted_iota(jnp.int32, (t, LANES), 1)

    def mask_queries(i, carry):
        r0 = pl.multiple_of(i * t, t)
        for hh in range(N_HEADS):
            qp = q_ref[pl.ds(r0, t), (hh // 2) * LANES:(hh // 2 + 1) * LANES]
            own = (lane >= (hh % 2) * HEAD_DIM) & (lane < (hh % 2 + 1) * HEAD_DIM)
            qm_ref[hh, pl.ds(r0, t), :] = jnp.where(own, qp, jnp.zeros_like(qp))
        return carry

    lax.fori_loop(0, nq, mask_queries, 0)
    qm_ref[:, nq * t:(nq + 1) * t, :] = jnp.zeros((N_HEADS, t, LANES), BF16)
    carry_ref[...] = jnp.zeros(carry_ref.shape, F32)
    acc_ref[...] = jnp.zeros(acc_ref.shape, F32)
    lb_ref[...] = jnp.zeros(lb_ref.shape, F32)
    later_ref[...] = jnp.zeros(later_ref.shape, F32)
    key_idx = lax.broadcasted_iota(jnp.int32, (SOFTMAX_ROWS, t), 0)
    qry_idx = lax.broadcasted_iota(jnp.int32, (SOFTMAX_ROWS, t), 1)

    def rows(c):
        return slice(c * SOFTMAX_ROWS, (c + 1) * SOFTMAX_ROWS)

    def logits(n, hh, z_ref):
        q0 = pl.multiple_of(qtab_ref[n] * t, t)
        k0 = pl.multiple_of(ktab_ref[n] * t, t)
        ps = slice((hh // 2) * LANES, (hh // 2 + 1) * LANES)
        z_ref[hh] = _dot_nt(k_ref[pl.ds(k0, t), ps], qm_ref[hh, pl.ds(q0, t), :])

    def log_terms(hh, z_ref, diagonal):
        for c in range(n_chunks):
            z = z_ref[hh, rows(c), :]
            softplus = jnp.log2(1.0 + jnp.exp2(-jnp.abs(z)))
            log_beta = jnp.minimum(z, 0.0) - softplus
            log_keep = log_beta - z
            if diagonal:
                log_keep = jnp.where(key_idx + c * SOFTMAX_ROWS < qry_idx, log_keep, 0.0)
            lb_ref[hh, rows(c), :] = log_beta
            lk_ref[hh, rows(c), :] = log_keep.astype(BF16)
        later_ref[hh] = _dot(tri_ref[...], lk_ref[hh])

    def weights_pv(n, hh, diagonal):
        qb = qtab_ref[n]
        carry = carry_ref[qb, hh]
        for c in range(n_chunks):
            w = jnp.exp2(lb_ref[hh, rows(c), :] + later_ref[hh, rows(c), :] + carry)
            if diagonal:
                w = jnp.where(key_idx + c * SOFTMAX_ROWS < qry_idx, w, 0.0)
            w_ref[hh, rows(c), :] = w.astype(BF16)
        carry_ref[qb, hh] = carry + later_ref[hh, t:t + 1, :]
        v_t = vt_ref[ktab_ref[n], hh * HEAD_DIM:(hh + 1) * HEAD_DIM, :]
        acc_ref[qb, hh] += _dot(v_t, w_ref[hh])

    tail_heads = range(N_HEADS - PIPE_DEPTH, N_HEADS)
    z_refs = (z0_ref, z1_ref)

    def trip(first, resumed, diagonal):
        pending = [(resumed, hh, diagonal) for hh in tail_heads]
        for i in range(ATTN_BODY_BLOCKS):
            for hh in range(N_HEADS):
                ahead = hh + PIPE_DEPTH
                if ahead < N_HEADS:
                    logits(first + i, ahead, z_refs[i % 2])
                else:
                    logits(first + i + 1, ahead - N_HEADS, z_refs[(i + 1) % 2])
                weights_pv(*pending.pop(0))
                log_terms(hh, z_refs[i % 2], diagonal)
                pending.append((first + i, hh, diagonal))

    def sweep(first, trips, diagonal):
        def body(r, carry):
            start = first + ATTN_BODY_BLOCKS * r
            trip(start, jnp.where(r == 0, 0, start - 1), diagonal)
            return carry

        lax.fori_loop(0, trips, body, 0)
        last = first + ATTN_BODY_BLOCKS * trips - 1
        for hh in tail_heads:
            weights_pv(last, hh, diagonal)

    for hh in range(PIPE_DEPTH):
        logits(1, hh, z0_ref)
    sweep(1, diag_trips, True)
    sweep(1 + ATTN_BODY_BLOCKS * diag_trips, off_trips, False)

    def write_out(i, carry):
        r0 = pl.multiple_of(i * t, t)
        y_t = jnp.concatenate([acc_ref[i, hh] for hh in range(N_HEADS)], axis=0)
        o_ref[pl.ds(r0, t), :] = _rms(y_t.T, gn_ref[...]).astype(BF16)
        return carry

    lax.fori_loop(0, nq, write_out, 0)


def _sb_attention(qk, v_t, gn, batch):
    m = qk.shape[0]
    s = m // batch
    t = ATTN_TILE
    nq = s // t
    idx = np.arange(t)
    later = idx[None, :] > idx[:, None]
    total = np.ones((SUM_ROWS, t), bool)
    tri = jnp.asarray(np.concatenate([later, total], axis=0), BF16)
    qtab, ktab, diag_trips, off_trips = _pair_order(nq)
    smem = pl.BlockSpec(memory_space=pltpu.SMEM)
    return pl.pallas_call(
        functools.partial(_sb_kernel, diag_trips=diag_trips, off_trips=off_trips),
        out_shape=jax.ShapeDtypeStruct((m, GROUP_W), BF16),
        grid=(batch,),
        in_specs=[smem, smem,
                  pl.BlockSpec((s, GROUP_W), lambda b: (b, 0)),
                  pl.BlockSpec((s, GROUP_W), lambda b: (b, 1)),
                  pl.BlockSpec((nq, GROUP_W, t), lambda b: (b, 0, 0)),
                  pl.BlockSpec(tri.shape, lambda b: (0, 0)),
                  pl.BlockSpec((1, GROUP_W), lambda b: (0, 0))],
        out_specs=pl.BlockSpec((s, GROUP_W), lambda b: (b, 0)),
        scratch_shapes=[pltpu.VMEM((N_HEADS, (nq + 1) * t, LANES), BF16),
                        pltpu.VMEM((nq + 1, N_HEADS, 1, t), F32),
                        pltpu.VMEM((nq + 1, N_HEADS, HEAD_DIM, t), F32),
                        pltpu.VMEM((N_HEADS, t, t), F32),
                        pltpu.VMEM((N_HEADS, t, t), F32),
                        pltpu.VMEM((N_HEADS, t, t), F32),
                        pltpu.VMEM((N_HEADS, t, t), BF16),
                        pltpu.VMEM((N_HEADS, t + SUM_ROWS, t), F32),
                        pltpu.VMEM((N_HEADS, t, t), BF16)],
        compiler_params=_params("arbitrary"),
        name="sb_attention",
    )(qtab, ktab, qk, qk, v_t, tri, gn)


def _lru_kernel(x_ref, gate_ref, cw_ref, cb_ref, wab_ref, ba_ref, bx_ref, lam_ref, gn_ref,
                o_ref, xs_ref, a_ref, u_ref, h_ref):
    ts, wide = x_ref.shape
    w = LRU_WIDTH
    seqs = range(wide // w)
    pad = CONV_WIDTH - 1
    hist = SUBLANES

    @pl.when(pl.program_id(0) == 0)
    def _():
        xs_ref[0:hist] = jnp.zeros((hist, wide), F32)
        h_ref[...] = jnp.zeros(h_ref.shape, F32)

    @pl.when(pl.program_id(0) > 0)
    def _():
        xs_ref[0:hist] = xs_ref[ts:ts + hist]

    xs_ref[hist:hist + ts] = x_ref[...]
    lam = lam_ref[...]
    softplus_neg_lam = jnp.maximum(-lam, 0.0) + jnp.log1p(jnp.exp(-jnp.abs(lam)))
    for b in seqs:
        sl = slice(b * w, (b + 1) * w)
        xc = cb_ref[...]
        for i in range(CONV_WIDTH):
            xc = xc + cw_ref[i:i + 1, :] * xs_ref[hist - pad + i:hist - pad + i + ts, sl]
        gates = _dot(xc.astype(BF16), wab_ref[...])
        rec_gate = _sigmoid(gates[:, :w] + ba_ref[...])
        in_gate = _sigmoid(gates[:, w:] + bx_ref[...])
        log_a = -LRU_C * rec_gate * softplus_neg_lam
        a = jnp.exp(log_a)
        a_ref[:, sl] = a
        one_minus_a2 = -jnp.tanh(log_a) * (a * a + 1.0)
        u_ref[:, sl] = jnp.sqrt(jnp.maximum(one_minus_a2, 0.0)) * (in_gate * xc)

    def step(i, h):
        h = a_ref[pl.ds(i, 1), :] * h + u_ref[pl.ds(i, 1), :]
        u_ref[pl.ds(i, 1), :] = h
        return h

    h_ref[...] = lax.fori_loop(0, ts, step, h_ref[...], unroll=8)

    for b in seqs:
        sl = slice(b * w, (b + 1) * w)
        g = gate_ref[:, sl]
        gelu = 0.5 * g * (1.0 + jnp.tanh(math.sqrt(2.0 / math.pi) * (g + 0.044715 * (g * g * g))))
        o_ref[:, sl] = _rms(u_ref[:, sl] * gelu, gn_ref[...])


def _rglru(x_t, gate_t, cw, cb, wab, ba, bx, lam, gn):
    s, wide = x_t.shape
    ts = LRU_TIME_TILE
    tile = pl.BlockSpec((ts, wide), lambda i: (i, 0))
    full = lambda a: pl.BlockSpec(a.shape, lambda i: (0,) * a.ndim)
    return pl.pallas_call(
        _lru_kernel,
        out_shape=jax.ShapeDtypeStruct((s, wide), F32),
        grid=(s // ts,),
        in_specs=[tile, tile, full(cw), full(cb), full(wab), full(ba), full(bx), full(lam), full(gn)],
        out_specs=tile,
        scratch_shapes=[pltpu.VMEM((ts + SUBLANES, wide), F32),
                        pltpu.VMEM((ts, wide), F32),
                        pltpu.VMEM((ts, wide), F32),
                        pltpu.VMEM((1, wide), F32)],
        compiler_params=_params("arbitrary"),
        name="rglru",
    )(x_t, gate_t, cw, cb, wab, ba, bx, lam, gn)


def _hg_constants():
    t = HG_CHUNK
    idx = np.arange(t)
    sums = [idx[:, None] >= idx[None, :]]
    masks = [idx[:, None] == idx[None, :]]
    for lev in range(HG_LEVELS):
        half = 1 << lev
        mid = (idx // (2 * half)) * (2 * half) + half
        upper = idx >= mid
        j = idx[None, :]
        up_rows = upper[:, None] & (j >= mid[:, None]) & (j <= idx[:, None])
        low_rows = (~upper)[:, None] & (j > idx[:, None]) & (j < mid[:, None])
        sums.append(up_rows | low_rows)
        same = (idx[:, None] // (2 * half)) == (idx[None, :] // (2 * half))
        masks.append(same & upper[:, None] & (~upper)[None, :])
    sum_mat = np.concatenate(sums, axis=0)
    sum_mat = jnp.asarray(np.concatenate([sum_mat, sum_mat], axis=1), BF16)
    mask = jnp.asarray(np.stack(masks, axis=0), F32)
    head = np.arange(GROUP_W) // HEAD_DIM
    same_head = head[:, None] == head[None, :]
    same_head = jnp.asarray(np.concatenate([same_head, same_head], axis=0), BF16)
    return sum_mat, mask, same_head


def _hg_kernel(x_ref, lbl_ref, ng_ref, sum_ref, mask_ref, same_ref, o_ref, st_ref, *, layer):
    t = HG_CHUNK
    w = GROUP_W

    seqs = range(x_ref.shape[0])

    @pl.when(pl.program_id(1) == 0)
    def _():
        st_ref[...] = jnp.zeros(st_ref.shape, F32)

    logits = lbl_ref[...]
    ex = jnp.exp(logits - jnp.max(logits, axis=0, keepdims=True))
    prob = ex / jnp.sum(ex, axis=0, keepdims=True)
    csum = prob[0:1, :]
    for i in range(1, layer + 1):
        csum = csum + prob[i:i + 1, :]
    lb = csum - prob[0:1, :]
    a_ = jnp.log(jnp.maximum(lb, LB_FLOOR))
    log_one_minus_lb = jnp.log1p(-lb)
    same = same_ref[...]
    lane_head = lax.broadcasted_iota(jnp.int32, (t, w), 1) // HEAD_DIM

    def by_head(x):
        return jnp.concatenate([jnp.where(lane_head == hh, x, jnp.zeros_like(x)) for hh in range(N_HEADS)], axis=0)

    q = [x_ref[b, :, 0:w] for b in seqs]
    vb = [x_ref[b, :, 2 * w:3 * w].astype(BF16) for b in seqs]
    key, sums = [], []
    for b in seqs:
        fz = x_ref[b, :, w:2 * w]
        b_ = log_one_minus_lb + _log_sigmoid(fz)
        log_f = jnp.maximum(a_, b_) + _log1p_of_fraction(jnp.exp(-jnp.abs(a_ - b_)))
        key.append((1.0 - lb) * _sigmoid(-fz))
        sums.append(_dot(sum_ref[...], jnp.concatenate(_split_bf16(log_f), axis=0)))

    out = []
    for b in seqs:
        cum = sums[b][0:t]
        last = cum[t - 1:t, :]
        st = st_ref[b]
        out.append(_dot_nt((q[b] * jnp.exp(cum)).astype(BF16), st.astype(BF16)))
        kdec = (key[b] * jnp.exp(last - cum)).astype(BF16)
        st_ref[b] = st * jnp.exp(last) + jnp.where(same[0:w] > 0, _dot_tn(vb[b], kdec), 0.0)

    scores = [[jnp.zeros((t, t), F32) for _ in range(N_HEADS)] for _ in seqs]
    for lev in range(HG_LEVELS + 1):
        for b in seqs:
            if lev == 0:
                qt, kt = q[b], key[b]
            else:
                decay = jnp.exp(sums[b][lev * t:(lev + 1) * t])
                qt, kt = q[b] * decay, key[b] * decay
            sc = _dot_nt(qt.astype(BF16), by_head(kt.astype(BF16)))
            for hh in range(N_HEADS):
                scores[b][hh] = scores[b][hh] + sc[:, hh * t:(hh + 1) * t] * mask_ref[lev]
    for b in seqs:
        all_scores = jnp.concatenate([s.astype(BF16) for s in scores[b]], axis=1)
        out[b] = out[b] + _dot(all_scores, by_head(vb[b]))

    mean_sq = [_dot(jnp.concatenate(_split_bf16(out[b] * out[b]), axis=1), same) * (1.0 / HEAD_DIM) for b in seqs]
    for b in seqs:
        g = x_ref[b, :, 3 * w:4 * w]
        normed = out[b] * lax.rsqrt(mean_sq[b] + NORM_EPS) * ng_ref[...]
        o_ref[b] = (normed * (g * _sigmoid(g))).astype(BF16)


def _hgrn2(x, lb_logits, ng, layer, batch):
    m = x.shape[0]
    t = HG_CHUNK
    nb = HG_SEQS
    seq = m // batch
    sum_mat, mask, same_head = _hg_constants()
    full = lambda a: pl.BlockSpec(a.shape, lambda b, c: (0,) * a.ndim)
    y = pl.pallas_call(
        functools.partial(_hg_kernel, layer=layer),
        out_shape=jax.ShapeDtypeStruct((batch, seq, GROUP_W), BF16),
        grid=(batch // nb, seq // t),
        in_specs=[pl.BlockSpec((nb, t, 4 * GROUP_W), lambda b, c: (b, c, 0)),
                  full(lb_logits), full(ng), full(sum_mat), full(mask), full(same_head)],
        out_specs=pl.BlockSpec((nb, t, GROUP_W), lambda b, c: (b, c, 0)),
        scratch_shapes=[pltpu.VMEM((nb, GROUP_W, GROUP_W), F32)],
        compiler_params=_params("arbitrary", "arbitrary"),
        name="hgrn2",
    )(x.reshape(batch, seq, 4 * GROUP_W), lb_logits, ng, sum_mat, mask, same_head)
    return y.reshape(m, GROUP_W)


def _out_ffn_kernel(h_ref, ya_ref, yb_ref, yc_ref, yd_ref, wout_ref, g2_ref, wg_ref, wu_ref, wd_ref,
                    gf_ref, o_ref, *, final):
    mixed = jnp.concatenate([ya_ref[...], yb_ref[...].astype(BF16), yc_ref[...], yd_ref[...]], axis=1)
    h1 = h_ref[...] + _dot(mixed, wout_ref[...])
    un = _rms(h1, g2_ref[...]).astype(BF16)
    chunks = [slice(c * FFN_CHUNK, (c + 1) * FFN_CHUNK) for c in range(FFN_HIDDEN // FFN_CHUNK)]
    gate_up = [(_dot(un, wg_ref[:, sl]), _dot(un, wu_ref[:, sl])) for sl in chunks]
    acc = h1
    for sl, (gt, up) in zip(chunks, gate_up):
        act = (gt * _sigmoid(gt) * up).astype(BF16)
        acc = acc + _dot(act, wd_ref[sl, :])
    if final:
        acc = _rms(acc, gf_ref[...])
    o_ref[...] = acc


def _out_ffn(h, ya, yb, yc, yd, wout, g2, wg, wu, wd, gf, layer, final):
    m = h.shape[0]
    tm = ROW_TILE
    rows = lambda w: pl.BlockSpec((tm, w), lambda i: (i, 0))

    def once(a):
        if a.ndim == 3:
            return pl.BlockSpec((None,) + a.shape[1:], lambda i: (layer, 0, 0), pipeline_mode=pl.Buffered(1))
        return pl.BlockSpec(a.shape, lambda i: (0, 0), pipeline_mode=pl.Buffered(1))
    tiles_per_seq = yb.shape[0] // tm
    yb_rows = pl.BlockSpec((tm, GROUP_W), lambda i: (i % tiles_per_seq, i // tiles_per_seq))
    return pl.pallas_call(
        functools.partial(_out_ffn_kernel, final=final),
        out_shape=jax.ShapeDtypeStruct((m, D_MODEL), F32),
        grid=(m // tm,),
        in_specs=[rows(D_MODEL), rows(GROUP_W), yb_rows, rows(GROUP_W), rows(GROUP_W),
                  once(wout), once(g2), once(wg), once(wu), once(wd), once(gf)],
        out_specs=rows(D_MODEL),
        compiler_params=_params("arbitrary"),
        name="out_ffn",
    )(h, ya, yb, yc, yd, wout, g2, wg, wu, wd, gf)


def _pad_in_proj(w):
    z = lambda n: jnp.zeros(w.shape[:-1] + (n,), w.dtype)
    sb_q = w[..., 928:928 + GROUP_W] * (LOG2_E * HEAD_DIM ** -0.5)
    return jnp.concatenate([w[..., :384], z(MLA_NOPE), w[..., 384:416], z(LANES - MLA_NOPE - MLA_ROPE),
                            w[..., 416:928], sb_q, w[..., 928 + GROUP_W:]], axis=-1).astype(BF16)


def _pad_heads(w, width):
    lead = w.shape[:-1]
    w = w.reshape(lead + (N_HEADS, width))
    w = jnp.pad(w, ((0, 0),) * (len(lead) + 1) + ((0, LANES - width),))
    return w.reshape(lead + (N_HEADS * LANES,))


def _block_diag(w):
    n, c, d = w.shape[-3:]
    eye = jnp.eye(n, dtype=w.dtype)
    return (eye[:, None, :, None] * w[..., :, :, None, :]).reshape(w.shape[:-3] + (n * c, n * d))


def kernel(x, positions, ln1_g, w_in, mla_q_norm_g, mla_w_uq, mla_kv_norm_g, mla_w_ukv, lru_conv_w, lru_conv_b, lru_w_a, lru_b_a, lru_w_x, lru_b_x, lru_lambda, hgrn_lb_logits, hgrn_norm_g, group_norm_g, w_out, ln2_g, w_ffn_gate, w_ffn_up, w_ffn_down, final_norm_g):
    batch, seq, d_model = x.shape
    depth = w_in.shape[0]
    m = batch * seq
    row = lambda v: v.reshape(1, -1)
    h = x.reshape(m, d_model)
    tabs = _rope_tables(positions)
    qk_scale = LOG2_E * (MLA_NOPE + MLA_ROPE) ** -0.5

    win = _pad_in_proj(w_in)
    wuq = _pad_heads(mla_w_uq * qk_scale, MLA_NOPE + MLA_ROPE).astype(BF16)
    wukv = mla_w_ukv.reshape(depth, MLA_KV_LORA, N_HEADS, MLA_NOPE + MLA_V)
    wuk = _pad_heads(wukv[..., :MLA_NOPE].reshape(depth, MLA_KV_LORA, -1), MLA_NOPE)
    wuv = wukv[..., MLA_NOPE:].reshape(depth, MLA_KV_LORA, -1)
    wukv = jnp.concatenate([wuk, wuv], axis=-1).astype(BF16)
    wab = jnp.concatenate([_block_diag(lru_w_a), _block_diag(lru_w_x)], axis=-1).astype(BF16)
    wout, wg, wu, wd = (w.astype(BF16) for w in (w_out, w_ffn_gate, w_ffn_up, w_ffn_down))

    for l in range(depth):
        qm, km, vm_t, lru_x, lru_gate, sb_qk, sb_vt, hg_in = _in_proj(
            h, row(ln1_g[l]), win, row(mla_q_norm_g[l]), wuq, row(mla_kv_norm_g[l]), wukv, tabs, l, batch)

        y_a = _mla_attention(qm, km, vm_t, row(group_norm_g[l, 0]), batch)
        y_b = _rglru(lru_x, lru_gate, lru_conv_w[l], row(lru_conv_b[l]), wab[l], row(lru_b_a[l]),
                     row(lru_b_x[l]), row(lru_lambda[l]), row(group_norm_g[l, 1]))

        y_c = _sb_attention(sb_qk, sb_vt, row(group_norm_g[l, 2]), batch)
        y_d = _hgrn2(hg_in, hgrn_lb_logits, row(hgrn_norm_g[l]), l, batch)

        h = _out_ffn(h, y_a, y_b, y_c, y_d, wout, row(ln2_g[l]), wg, wu, wd, row(final_norm_g),
                     l, final=(l == depth - 1))
    return h.reshape(batch, seq, d_model)
```

```python
import functools
import math

import numpy as np
import jax
import jax.numpy as jnp
from jax import lax
from jax.experimental import pallas as pl
from jax.experimental.pallas import tpu as pltpu

F32 = jnp.float32
BF16 = jnp.bfloat16

D_MODEL = 1024
N_HEADS = 4
MLA_Q_LORA = 256
MLA_KV_LORA = 128
MLA_NOPE = 64
MLA_ROPE = 32
MLA_V = 64
ROPE_THETA = 10000.0
LRU_WIDTH = 256
CONV_WIDTH = 4
LRU_C = 8.0
HEAD_DIM = 64
GROUP_W = 256
FFN_HIDDEN = 2816
NORM_EPS = 1e-6
MASK_VALUE = -1e30
LB_FLOOR = 1e-30

LANES = 128
SUBLANES = 8
VMEM_LIMIT_BYTES = 56 * 1024 * 1024

ROW_TILE = 512
ATTN_TILE = 256
SOFTMAX_ROWS = 64
SUM_ROWS = 16
PIPE_DEPTH = 2
ATTN_BODY_BLOCKS = 8
LOG2_E = math.log2(math.e)
LRU_TIME_TILE = 256
HG_CHUNK = 128
HG_LEVELS = int(math.log2(HG_CHUNK))
HG_SEQS = 4
FFN_CHUNK = FFN_HIDDEN

IN_MLA = (0, 512)
IN_LRU = (512, 1024)
IN_SB = (1024, 1792)
IN_HG = (1792, 2816)


def _params(*semantics):
    return pltpu.CompilerParams(dimension_semantics=semantics, vmem_limit_bytes=VMEM_LIMIT_BYTES)


def _rms(x, g):
    return x * lax.rsqrt(jnp.mean(x * x, axis=-1, keepdims=True) + NORM_EPS) * g


def _rms_columns(x_t, g_col):
    return x_t * lax.rsqrt(jnp.mean(x_t * x_t, axis=0, keepdims=True) + NORM_EPS) * g_col


def _dot(a, b):
    return jnp.dot(a, b, preferred_element_type=F32)


def _dot_nt(a, b):
    return lax.dot_general(a, b, (((1,), (1,)), ((), ())), preferred_element_type=F32)


def _dot_tn(a, b):
    return lax.dot_general(a, b, (((0,), (0,)), ((), ())), preferred_element_type=F32)


def _split_bf16(x):
    hi = x.astype(BF16)
    lo = (x - hi.astype(F32)).astype(BF16)
    return hi, lo


def _sigmoid(x):
    return 1.0 / (1.0 + jnp.exp(-x))


def _log1p_of_fraction(y):
    return jnp.log(1.0 + y)


def _log_sigmoid(x):
    return jnp.minimum(x, 0.0) - _log1p_of_fraction(jnp.exp(-jnp.abs(x)))


def _rope_table_kernel(pos_ref, freq_ref, m1_ref, m2_ref, c_ref, s1_ref, s2_ref):
    ang = pos_ref[...] * freq_ref[...]
    sn = jnp.sin(ang)
    c_ref[...] = jnp.cos(ang)
    s1_ref[...] = sn * m1_ref[...]
    s2_ref[...] = sn * m2_ref[...]


def _rope_tables(positions):
    m = positions.size
    half = MLA_ROPE // 2
    inv_freq = ROPE_THETA ** (-jnp.arange(half, dtype=F32) / half)
    zeros = jnp.zeros((MLA_NOPE,), F32)
    tail = jnp.zeros((LANES - MLA_NOPE - MLA_ROPE,), F32)
    freq = jnp.concatenate([zeros, inv_freq, inv_freq, tail]).reshape(1, LANES)
    ones = jnp.ones((half,), F32)
    m1 = jnp.concatenate([zeros, -ones, 0 * ones, tail]).reshape(1, LANES)
    m2 = jnp.concatenate([zeros, 0 * ones, ones, tail]).reshape(1, LANES)
    pos = positions.astype(F32).reshape(m, 1)
    tm = ROW_TILE
    row = pl.BlockSpec((1, LANES), lambda i: (0, 0))
    tab = pl.BlockSpec((tm, LANES), lambda i: (i, 0))
    shape = jax.ShapeDtypeStruct((m, LANES), F32)
    return pl.pallas_call(
        _rope_table_kernel,
        out_shape=(shape, shape, shape),
        grid=(m // tm,),
        in_specs=[pl.BlockSpec((tm, 1), lambda i: (i, 0)), row, row, row],
        out_specs=(tab, tab, tab),
        compiler_params=_params("arbitrary"),
        name="rope_tables",
    )(pos, freq, m1, m2)


def _in_proj_kernel(h_ref, g1_ref, win_ref, qg_ref, wuq_ref, kvg_ref, wukv_ref,
                    c_ref, s1_ref, s2_ref,
                    qm_ref, km_ref, vmt_ref, lrux_ref, lrug_ref, sb_ref, sbvt_ref, hg_ref):
    t = ATTN_TILE
    n_blocks = h_ref.shape[0] // t

    def store_transposed(ref, v):
        for blk in range(n_blocks):
            ref[blk] = v[blk * t:(blk + 1) * t, :].T.astype(BF16)

    u = _rms(h_ref[...], g1_ref[...]).astype(BF16)
    p = _dot(u, win_ref[:, IN_MLA[0]:IN_MLA[1]])
    lrux_ref[...] = _dot(u, win_ref[:, IN_LRU[0]:IN_LRU[0] + LRU_WIDTH])
    lrug_ref[...] = _dot(u, win_ref[:, IN_LRU[0] + LRU_WIDTH:IN_LRU[1]])
    cq = _rms(p[:, 0:MLA_Q_LORA], qg_ref[...]).astype(BF16)
    ckv = _rms(p[:, MLA_Q_LORA:MLA_Q_LORA + MLA_KV_LORA], kvg_ref[...]).astype(BF16)
    q = _dot(cq, wuq_ref[...])
    kv = _dot(ckv, wukv_ref[...])
    sb = _dot(u, win_ref[:, IN_SB[0]:IN_SB[1]])
    c, s1, s2 = c_ref[...], s1_ref[...], s2_ref[...]

    def rope(x):
        return x * c + pltpu.roll(x, LANES - MLA_ROPE // 2, 1) * s1 + pltpu.roll(x, MLA_ROPE // 2, 1) * s2

    kpe = rope(p[:, 384:512])
    for hh in range(N_HEADS):
        sl = slice(hh * LANES, (hh + 1) * LANES)
        qm_ref[:, sl] = rope(q[:, sl]).astype(BF16)
        km_ref[:, sl] = (kv[:, sl] + kpe).astype(BF16)
    store_transposed(vmt_ref, kv[:, N_HEADS * LANES:])
    hg_ref[...] = _dot(u, win_ref[:, IN_HG[0]:IN_HG[1]])
    sb_ref[...] = sb[:, :2 * GROUP_W].astype(BF16)
    store_transposed(sbvt_ref, sb[:, 2 * GROUP_W:])


def _in_proj(h, g1, win, qg, wuq, kvg, wukv, tabs, layer, batch):
    m = h.shape[0]
    seq = m // batch
    tm = ROW_TILE
    t = ATTN_TILE
    rows = lambda w: pl.BlockSpec((tm, w), lambda i: (i, 0))

    def full(a):
        if a.ndim == 3:
            return pl.BlockSpec((None,) + a.shape[1:], lambda i: (layer, 0, 0))
        return pl.BlockSpec(a.shape, lambda i: (0, 0))
    c, s1, s2 = tabs
    v_t = jax.ShapeDtypeStruct((m // t, GROUP_W, t), BF16)
    out_shape = (
        jax.ShapeDtypeStruct((m, N_HEADS * LANES), BF16),
        jax.ShapeDtypeStruct((m, N_HEADS * LANES), BF16),
        v_t,
        jax.ShapeDtypeStruct((seq, batch * LRU_WIDTH), F32),
        jax.ShapeDtypeStruct((seq, batch * LRU_WIDTH), F32),
        jax.ShapeDtypeStruct((m, 2 * GROUP_W), BF16),
        v_t,
        jax.ShapeDtypeStruct((m, 4 * GROUP_W), F32),
    )
    tiles_per_seq = seq // tm

    def out_spec(sd):
        if sd.shape[0] == seq:
            return pl.BlockSpec((tm, LRU_WIDTH), lambda i: (i % tiles_per_seq, i // tiles_per_seq))
        if len(sd.shape) == 3:
            return pl.BlockSpec((tm // t, GROUP_W, t), lambda i: (i, 0, 0))
        return rows(sd.shape[1])
    return pl.pallas_call(
        _in_proj_kernel,
        out_shape=out_shape,
        grid=(m // tm,),
        in_specs=[rows(D_MODEL), full(g1), full(win), full(qg), full(wuq), full(kvg), full(wukv),
                  rows(LANES), rows(LANES), rows(LANES)],
        out_specs=tuple(out_spec(sd) for sd in out_shape),
        compiler_params=_params("arbitrary"),
        name="in_proj",
    )(h, g1, win, qg, wuq, kvg, wukv, c, s1, s2)


def _pair_order(nq):
    dummy = (nq, 0)
    pad = lambda pairs: pairs + [dummy] * (-len(pairs) % ATTN_BODY_BLOCKS)
    diag = pad([(i, i) for i in range(nq)])
    off = pad([(i, i - d) for d in range(1, nq) for i in range(d, nq)])
    order = np.array([dummy] + diag + off + [dummy] * 2, np.int32)
    return (jnp.asarray(order[:, 0]), jnp.asarray(order[:, 1]),
            len(diag) // ATTN_BODY_BLOCKS, len(off) // ATTN_BODY_BLOCKS)


def _mla_kernel(qtab_ref, ktab_ref, q_ref, k_ref, vt_ref, gn_ref, o_ref,
                m_ref, acc_ref, s0_ref, s1_ref, p_ref, alpha_ref, *, diag_trips, off_trips):
    t = ATTN_TILE
    nq = q_ref.shape[0] // t
    m_ref[...] = jnp.full(m_ref.shape, MASK_VALUE, F32)
    acc_ref[...] = jnp.zeros(acc_ref.shape, F32)
    n_chunks = t // SOFTMAX_ROWS
    key_idx = lax.broadcasted_iota(jnp.int32, (SOFTMAX_ROWS, t), 0)
    qry_idx = lax.broadcasted_iota(jnp.int32, (SOFTMAX_ROWS, t), 1)
    ones_rows = jnp.ones((SUM_ROWS, t), BF16)
    s_refs = (s0_ref, s1_ref)

    def scores(n, hh, s_ref):
        q0 = pl.multiple_of(jnp.minimum(qtab_ref[n], nq - 1) * t, t)
        k0 = pl.multiple_of(ktab_ref[n] * t, t)
        sl = slice(hh * LANES, (hh + 1) * LANES)
        s_ref[hh] = _dot_nt(k_ref[pl.ds(k0, t), sl], q_ref[pl.ds(q0, t), sl])

    def softmax(n, hh, s_ref, diagonal):
        def chunk(c):
            s = s_ref[hh, c * SOFTMAX_ROWS:(c + 1) * SOFTMAX_ROWS, :]
            if diagonal:
                s = jnp.where(key_idx + c * SOFTMAX_ROWS <= qry_idx, s, MASK_VALUE)
            return s

        qb = qtab_ref[n]
        m_prev = m_ref[qb, hh]
        m_blk = chunk(0)
        for c in range(1, n_chunks):
            m_blk = jnp.maximum(m_blk, chunk(c))
        m_new = jnp.maximum(m_prev, jnp.max(m_blk, axis=0, keepdims=True))
        alpha_ref[hh] = jnp.exp2(m_prev - m_new)
        for c in range(n_chunks):
            p_ref[hh, c * SOFTMAX_ROWS:(c + 1) * SOFTMAX_ROWS, :] = jnp.exp2(chunk(c) - m_new).astype(BF16)
        m_ref[qb, hh] = m_new

    def weighted_values(n, hh):
        qb = qtab_ref[n]
        v_ext = jnp.concatenate([vt_ref[ktab_ref[n], hh * MLA_V:(hh + 1) * MLA_V, :], ones_rows], axis=0)
        acc_ref[qb, hh] = alpha_ref[hh] * acc_ref[qb, hh] + _dot(v_ext, p_ref[hh])

    def sweep(first, trips, diagonal):
        def body(r, carry):
            start = first + ATTN_BODY_BLOCKS * r
            previous = (jnp.where(r == 0, 0, start - 1), N_HEADS - 1)
            for i in range(ATTN_BODY_BLOCKS):
                for hh in range(N_HEADS):
                    scores(start + i + 1, hh, s_refs[(i + 1) % 2])
                    weighted_values(*previous)
                    softmax(start + i, hh, s_refs[i % 2], diagonal)
                    previous = (start + i, hh)
            return carry

        lax.fori_loop(0, trips, body, 0)
        weighted_values(first + ATTN_BODY_BLOCKS * trips - 1, N_HEADS - 1)

    alpha_ref[...] = jnp.zeros(alpha_ref.shape, F32)
    p_ref[...] = jnp.zeros(p_ref.shape, BF16)
    for hh in range(N_HEADS):
        scores(1, hh, s0_ref)
    sweep(1, diag_trips, True)
    sweep(1 + ATTN_BODY_BLOCKS * diag_trips, off_trips, False)

    def write_out(i, carry):
        y_t = jnp.concatenate([acc_ref[i, hh, 0:MLA_V, :] / acc_ref[i, hh, MLA_V:MLA_V + 1, :]
                               for hh in range(N_HEADS)], axis=0)
        o_ref[i] = _rms_columns(y_t, gn_ref[...])
        return carry

    lax.fori_loop(0, nq, write_out, 0)


def _mla_attention(q, k, v_t, gn, batch):
    m = q.shape[0]
    s = m // batch
    t = ATTN_TILE
    nq = s // t
    qtab, ktab, diag_trips, off_trips = _pair_order(nq)
    smem = pl.BlockSpec(memory_space=pltpu.SMEM)
    return pl.pallas_call(
        functools.partial(_mla_kernel, diag_trips=diag_trips, off_trips=off_trips),
        out_shape=jax.ShapeDtypeStruct((m // t, GROUP_W, t), F32),
        grid=(batch,),
        in_specs=[smem, smem,
                  pl.BlockSpec((s, N_HEADS * LANES), lambda b: (b, 0)),
                  pl.BlockSpec((s, N_HEADS * LANES), lambda b: (b, 0)),
                  pl.BlockSpec((nq, GROUP_W, t), lambda b: (b, 0, 0)),
                  pl.BlockSpec((GROUP_W, 1), lambda b: (0, 0))],
        out_specs=pl.BlockSpec((nq, GROUP_W, t), lambda b: (b, 0, 0)),
        scratch_shapes=[pltpu.VMEM((nq + 1, N_HEADS, 1, t), F32),
                        pltpu.VMEM((nq + 1, N_HEADS, MLA_V + SUM_ROWS, t), F32),
                        pltpu.VMEM((N_HEADS, t, t), F32),
                        pltpu.VMEM((N_HEADS, t, t), F32),
                        pltpu.VMEM((N_HEADS, t, t), BF16),
                        pltpu.VMEM((N_HEADS, 1, t), F32)],
        compiler_params=_params("arbitrary"),
        name="mla_attention",
    )(qtab, ktab, q, k, v_t, gn)


def _sb_kernel(qtab_ref, ktab_ref, q_ref, k_ref, vt_ref, tri_ref, gn_ref, o_ref,
               qm_ref, carry_ref, acc_ref, z0_ref, z1_ref, lb_ref, lk_ref, later_ref, w_ref,
               *, diag_trips, off_trips):
    t = ATTN_TILE
    nq = q_ref.shape[0] // t
    n_chunks = t // SOFTMAX_ROWS
    lane = lax.broadcasted_iota(jnp.int32, (t, LANES), 1)

    def mask_queries(i, carry):
        r0 = pl.multiple_of(i * t, t)
        for hh in range(N_HEADS):
            qp = q_ref[pl.ds(r0, t), (hh // 2) * LANES:(hh // 2 + 1) * LANES]
            own = (lane >= (hh % 2) * HEAD_DIM) & (lane < (hh % 2 + 1) * HEAD_DIM)
            qm_ref[hh, pl.ds(r0, t), :] = jnp.where(own, qp, jnp.zeros_like(qp))
        return carry

    lax.fori_loop(0, nq, mask_queries, 0)
    qm_ref[:, nq * t:(nq + 1) * t, :] = jnp.zeros((N_HEADS, t, LANES), BF16)
    carry_ref[...] = jnp.zeros(carry_ref.shape, F32)
    acc_ref[...] = jnp.zeros(acc_ref.shape, F32)
    lb_ref[...] = jnp.zeros(lb_ref.shape, F32)
    later_ref[...] = jnp.zeros(later_ref.shape, F32)
    key_idx = lax.broadcasted_iota(jnp.int32, (SOFTMAX_ROWS, t), 0)
    qry_idx = lax.broadcasted_iota(jnp.int32, (SOFTMAX_ROWS, t), 1)

    def rows(c):
        return slice(c * SOFTMAX_ROWS, (c + 1) * SOFTMAX_ROWS)

    def logits(n, hh, z_ref):
        q0 = pl.multiple_of(qtab_ref[n] * t, t)
        k0 = pl.multiple_of(ktab_ref[n] * t, t)
        ps = slice((hh // 2) * LANES, (hh // 2 + 1) * LANES)
        z_ref[hh] = _dot_nt(k_ref[pl.ds(k0, t), ps], qm_ref[hh, pl.ds(q0, t), :])

    def log_terms(hh, z_ref, diagonal):
        for c in range(n_chunks):
            z = z_ref[hh, rows(c), :]
            softplus = jnp.log2(1.0 + jnp.exp2(-jnp.abs(z)))
            log_beta = jnp.minimum(z, 0.0) - softplus
            log_keep = log_beta - z
            if diagonal:
                log_keep = jnp.where(key_idx + c * SOFTMAX_ROWS < qry_idx, log_keep, 0.0)
            lb_ref[hh, rows(c), :] = log_beta
            lk_ref[hh, rows(c), :] = log_keep.astype(BF16)
        later_ref[hh] = _dot(tri_ref[...], lk_ref[hh])

    def weights_pv(n, hh, diagonal):
        qb = qtab_ref[n]
        carry = carry_ref[qb, hh]
        for c in range(n_chunks):
            w = jnp.exp2(lb_ref[hh, rows(c), :] + later_ref[hh, rows(c), :] + carry)
            if diagonal:
                w = jnp.where(key_idx + c * SOFTMAX_ROWS < qry_idx, w, 0.0)
            w_ref[hh, rows(c), :] = w.astype(BF16)
        carry_ref[qb, hh] = carry + later_ref[hh, t:t + 1, :]
        v_t = vt_ref[ktab_ref[n], hh * HEAD_DIM:(hh + 1) * HEAD_DIM, :]
        acc_ref[qb, hh] += _dot(v_t, w_ref[hh])

    tail_heads = range(N_HEADS - PIPE_DEPTH, N_HEADS)
    z_refs = (z0_ref, z1_ref)

    def trip(first, resumed, diagonal):
        pending = [(resumed, hh, diagonal) for hh in tail_heads]
        for i in range(ATTN_BODY_BLOCKS):
            for hh in range(N_HEADS):
                ahead = hh + PIPE_DEPTH
                if ahead < N_HEADS:
                    logits(first + i, ahead, z_refs[i % 2])
                else:
                    logits(first + i + 1, ahead - N_HEADS, z_refs[(i + 1) % 2])
                weights_pv(*pending.pop(0))
                log_terms(hh, z_refs[i % 2], diagonal)
                pending.append((first + i, hh, diagonal))

    def sweep(first, trips, diagonal):
        def body(r, carry):
            start = first + ATTN_BODY_BLOCKS * r
            trip(start, jnp.where(r == 0, 0, start - 1), diagonal)
            return carry

        lax.fori_loop(0, trips, body, 0)
        last = first + ATTN_BODY_BLOCKS * trips - 1
        for hh in tail_heads:
            weights_pv(last, hh, diagonal)

    for hh in range(PIPE_DEPTH):
        logits(1, hh, z0_ref)
    sweep(1, diag_trips, True)
    sweep(1 + ATTN_BODY_BLOCKS * diag_trips, off_trips, False)

    def write_out(i, carry):
        y_t = jnp.concatenate([acc_ref[i, hh] for hh in range(N_HEADS)], axis=0)
        o_ref[i] = _rms_columns(y_t, gn_ref[...])
        return carry

    lax.fori_loop(0, nq, write_out, 0)


def _sb_attention(qk, v_t, gn, batch):
    m = qk.shape[0]
    s = m // batch
    t = ATTN_TILE
    nq = s // t
    idx = np.arange(t)
    later = idx[None, :] > idx[:, None]
    total = np.ones((SUM_ROWS, t), bool)
    tri = jnp.asarray(np.concatenate([later, total], axis=0), BF16)
    qtab, ktab, diag_trips, off_trips = _pair_order(nq)
    smem = pl.BlockSpec(memory_space=pltpu.SMEM)
    return pl.pallas_call(
        functools.partial(_sb_kernel, diag_trips=diag_trips, off_trips=off_trips),
        out_shape=jax.ShapeDtypeStruct((m // t, GROUP_W, t), F32),
        grid=(batch,),
        in_specs=[smem, smem,
                  pl.BlockSpec((s, GROUP_W), lambda b: (b, 0)),
                  pl.BlockSpec((s, GROUP_W), lambda b: (b, 1)),
                  pl.BlockSpec((nq, GROUP_W, t), lambda b: (b, 0, 0)),
                  pl.BlockSpec(tri.shape, lambda b: (0, 0)),
                  pl.BlockSpec((GROUP_W, 1), lambda b: (0, 0))],
        out_specs=pl.BlockSpec((nq, GROUP_W, t), lambda b: (b, 0, 0)),
        scratch_shapes=[pltpu.VMEM((N_HEADS, (nq + 1) * t, LANES), BF16),
                        pltpu.VMEM((nq + 1, N_HEADS, 1, t), F32),
                        pltpu.VMEM((nq + 1, N_HEADS, HEAD_DIM, t), F32),
                        pltpu.VMEM((N_HEADS, t, t), F32),
                        pltpu.VMEM((N_HEADS, t, t), F32),
                        pltpu.VMEM((N_HEADS, t, t), F32),
                        pltpu.VMEM((N_HEADS, t, t), BF16),
                        pltpu.VMEM((N_HEADS, t + SUM_ROWS, t), F32),
                        pltpu.VMEM((N_HEADS, t, t), BF16)],
        compiler_params=_params("arbitrary"),
        name="sb_attention",
    )(qtab, ktab, qk, qk, v_t, tri, gn)


def _lru_kernel(x_ref, gate_ref, cw_ref, cb_ref, wab_ref, ba_ref, bx_ref, lam_ref, gn_ref,
                o_ref, xs_ref, a_ref, u_ref, h_ref):
    ts, wide = x_ref.shape
    w = LRU_WIDTH
    seqs = range(wide // w)
    pad = CONV_WIDTH - 1
    hist = SUBLANES

    @pl.when(pl.program_id(0) == 0)
    def _():
        xs_ref[0:hist] = jnp.zeros((hist, wide), F32)
        h_ref[...] = jnp.zeros(h_ref.shape, F32)

    @pl.when(pl.program_id(0) > 0)
    def _():
        xs_ref[0:hist] = xs_ref[ts:ts + hist]

    xs_ref[hist:hist + ts] = x_ref[...]
    lam = lam_ref[...]
    softplus_neg_lam = jnp.maximum(-lam, 0.0) + jnp.log1p(jnp.exp(-jnp.abs(lam)))
    for b in seqs:
        sl = slice(b * w, (b + 1) * w)
        xc = cb_ref[...]
        for i in range(CONV_WIDTH):
            xc = xc + cw_ref[i:i + 1, :] * xs_ref[hist - pad + i:hist - pad + i + ts, sl]
        gates = _dot(xc.astype(BF16), wab_ref[...])
        rec_gate = _sigmoid(gates[:, :w] + ba_ref[...])
        in_gate = _sigmoid(gates[:, w:] + bx_ref[...])
        log_a = -LRU_C * rec_gate * softplus_neg_lam
        a = jnp.exp(log_a)
        a_ref[:, sl] = a
        one_minus_a2 = -jnp.tanh(log_a) * (a * a + 1.0)
        u_ref[:, sl] = jnp.sqrt(jnp.maximum(one_minus_a2, 0.0)) * (in_gate * xc)

    def step(i, h):
        h = a_ref[pl.ds(i, 1), :] * h + u_ref[pl.ds(i, 1), :]
        u_ref[pl.ds(i, 1), :] = h
        return h

    h_ref[...] = lax.fori_loop(0, ts, step, h_ref[...], unroll=8)

    for b in seqs:
        sl = slice(b * w, (b + 1) * w)
        g = gate_ref[:, sl]
        gelu = 0.5 * g * (1.0 + jnp.tanh(math.sqrt(2.0 / math.pi) * (g + 0.044715 * (g * g * g))))
        o_ref[:, sl] = _rms(u_ref[:, sl] * gelu, gn_ref[...])


def _rglru(x_t, gate_t, cw, cb, wab, ba, bx, lam, gn):
    s, wide = x_t.shape
    ts = LRU_TIME_TILE
    tile = pl.BlockSpec((ts, wide), lambda i: (i, 0))
    full = lambda a: pl.BlockSpec(a.shape, lambda i: (0,) * a.ndim)
    return pl.pallas_call(
        _lru_kernel,
        out_shape=jax.ShapeDtypeStruct((s, wide), F32),
        grid=(s // ts,),
        in_specs=[tile, tile, full(cw), full(cb), full(wab), full(ba), full(bx), full(lam), full(gn)],
        out_specs=tile,
        scratch_shapes=[pltpu.VMEM((ts + SUBLANES, wide), F32),
                        pltpu.VMEM((ts, wide), F32),
                        pltpu.VMEM((ts, wide), F32),
                        pltpu.VMEM((1, wide), F32)],
        compiler_params=_params("arbitrary"),
        name="rglru",
    )(x_t, gate_t, cw, cb, wab, ba, bx, lam, gn)


def _hg_constants():
    t = HG_CHUNK
    idx = np.arange(t)
    sums = [idx[:, None] >= idx[None, :]]
    masks = [idx[:, None] == idx[None, :]]
    for lev in range(HG_LEVELS):
        half = 1 << lev
        mid = (idx // (2 * half)) * (2 * half) + half
        upper = idx >= mid
        j = idx[None, :]
        up_rows = upper[:, None] & (j >= mid[:, None]) & (j <= idx[:, None])
        low_rows = (~upper)[:, None] & (j > idx[:, None]) & (j < mid[:, None])
        sums.append(up_rows | low_rows)
        same = (idx[:, None] // (2 * half)) == (idx[None, :] // (2 * half))
        masks.append(same & upper[:, None] & (~upper)[None, :])
    sum_mat = np.concatenate(sums, axis=0)
    sum_mat = jnp.asarray(np.concatenate([sum_mat, sum_mat], axis=1), BF16)
    mask = jnp.asarray(np.stack(masks, axis=0), F32)
    head = np.arange(GROUP_W) // HEAD_DIM
    same_head = head[:, None] == head[None, :]
    same_head = jnp.asarray(np.concatenate([same_head, same_head], axis=0), BF16)
    return sum_mat, mask, same_head


def _hg_kernel(x_ref, lbl_ref, ng_ref, sum_ref, mask_ref, same_ref, o_ref, st_ref, *, layer):
    t = HG_CHUNK
    w = GROUP_W

    seqs = range(x_ref.shape[0])

    @pl.when(pl.program_id(1) == 0)
    def _():
        st_ref[...] = jnp.zeros(st_ref.shape, F32)

    logits = lbl_ref[...]
    ex = jnp.exp(logits - jnp.max(logits, axis=0, keepdims=True))
    prob = ex / jnp.sum(ex, axis=0, keepdims=True)
    csum = prob[0:1, :]
    for i in range(1, layer + 1):
        csum = csum + prob[i:i + 1, :]
    lb = csum - prob[0:1, :]
    a_ = jnp.log(jnp.maximum(lb, LB_FLOOR))
    log_one_minus_lb = jnp.log1p(-lb)
    same = same_ref[...]
    lane_head = lax.broadcasted_iota(jnp.int32, (t, w), 1) // HEAD_DIM

    def by_head(x):
        return jnp.concatenate([jnp.where(lane_head == hh, x, jnp.zeros_like(x)) for hh in range(N_HEADS)], axis=0)

    q = [x_ref[b, :, 0:w] for b in seqs]
    vb = [x_ref[b, :, 2 * w:3 * w].astype(BF16) for b in seqs]
    key, sums = [], []
    for b in seqs:
        fz = x_ref[b, :, w:2 * w]
        b_ = log_one_minus_lb + _log_sigmoid(fz)
        log_f = jnp.maximum(a_, b_) + _log1p_of_fraction(jnp.exp(-jnp.abs(a_ - b_)))
        key.append((1.0 - lb) * _sigmoid(-fz))
        sums.append(_dot(sum_ref[...], jnp.concatenate(_split_bf16(log_f), axis=0)))

    out = []
    for b in seqs:
        cum = sums[b][0:t]
        last = cum[t - 1:t, :]
        st = st_ref[b]
        out.append(_dot_nt((q[b] * jnp.exp(cum)).astype(BF16), st.astype(BF16)))
        kdec = (key[b] * jnp.exp(last - cum)).astype(BF16)
        st_ref[b] = st * jnp.exp(last) + jnp.where(same[0:w] > 0, _dot_tn(vb[b], kdec), 0.0)

    scores = [[jnp.zeros((t, t), F32) for _ in range(N_HEADS)] for _ in seqs]
    for lev in range(HG_LEVELS + 1):
        for b in seqs:
            if lev == 0:
                qt, kt = q[b], key[b]
            else:
                decay = jnp.exp(sums[b][lev * t:(lev + 1) * t])
                qt, kt = q[b] * decay, key[b] * decay
            sc = _dot_nt(qt.astype(BF16), by_head(kt.astype(BF16)))
            for hh in range(N_HEADS):
                scores[b][hh] = scores[b][hh] + sc[:, hh * t:(hh + 1) * t] * mask_ref[lev]
    for b in seqs:
        all_scores = jnp.concatenate([s.astype(BF16) for s in scores[b]], axis=1)
        out[b] = out[b] + _dot(all_scores, by_head(vb[b]))

    mean_sq = [_dot(jnp.concatenate(_split_bf16(out[b] * out[b]), axis=1), same) * (1.0 / HEAD_DIM) for b in seqs]
    for b in seqs:
        g = x_ref[b, :, 3 * w:4 * w]
        normed = out[b] * lax.rsqrt(mean_sq[b] + NORM_EPS) * ng_ref[...]
        o_ref[b] = (normed * (g * _sigmoid(g))).astype(BF16)


def _hgrn2(x, lb_logits, ng, layer, batch):
    m = x.shape[0]
    t = HG_CHUNK
    nb = HG_SEQS
    seq = m // batch
    sum_mat, mask, same_head = _hg_constants()
    full = lambda a: pl.BlockSpec(a.shape, lambda b, c: (0,) * a.ndim)
    y = pl.pallas_call(
        functools.partial(_hg_kernel, layer=layer),
        out_shape=jax.ShapeDtypeStruct((batch, seq, GROUP_W), BF16),
        grid=(batch // nb, seq // t),
        in_specs=[pl.BlockSpec((nb, t, 4 * GROUP_W), lambda b, c: (b, c, 0)),
                  full(lb_logits), full(ng), full(sum_mat), full(mask), full(same_head)],
        out_specs=pl.BlockSpec((nb, t, GROUP_W), lambda b, c: (b, c, 0)),
        scratch_shapes=[pltpu.VMEM((nb, GROUP_W, GROUP_W), F32)],
        compiler_params=_params("arbitrary", "arbitrary"),
        name="hgrn2",
    )(x.reshape(batch, seq, 4 * GROUP_W), lb_logits, ng, sum_mat, mask, same_head)
    return y.reshape(m, GROUP_W)


def _out_ffn_kernel(h_ref, ya_ref, yb_ref, yc_ref, yd_ref, wout_ref, g2_ref, wg_ref, wu_ref, wd_ref,
                    gf_ref, o_ref, *, final):
    def rows_of(yt_ref):
        return jnp.concatenate([yt_ref[blk].T for blk in range(yt_ref.shape[0])], axis=0).astype(BF16)

    mixed = jnp.concatenate([rows_of(ya_ref), yb_ref[...].astype(BF16), rows_of(yc_ref), yd_ref[...]], axis=1)
    h1 = h_ref[...] + _dot(mixed, wout_ref[...])
    un = _rms(h1, g2_ref[...]).astype(BF16)
    chunks = [slice(c * FFN_CHUNK, (c + 1) * FFN_CHUNK) for c in range(FFN_HIDDEN // FFN_CHUNK)]
    gate_up = [(_dot(un, wg_ref[:, sl]), _dot(un, wu_ref[:, sl])) for sl in chunks]
    acc = h1
    for sl, (gt, up) in zip(chunks, gate_up):
        act = (gt * _sigmoid(gt) * up).astype(BF16)
        acc = acc + _dot(act, wd_ref[sl, :])
    if final:
        acc = _rms(acc, gf_ref[...])
    o_ref[...] = acc


def _out_ffn(h, ya, yb, yc, yd, wout, g2, wg, wu, wd, gf, layer, final):
    m = h.shape[0]
    tm = ROW_TILE
    rows = lambda w: pl.BlockSpec((tm, w), lambda i: (i, 0))

    def once(a):
        if a.ndim == 3:
            return pl.BlockSpec((None,) + a.shape[1:], lambda i: (layer, 0, 0), pipeline_mode=pl.Buffered(1))
        return pl.BlockSpec(a.shape, lambda i: (0, 0), pipeline_mode=pl.Buffered(1))
    tiles_per_seq = yb.shape[0] // tm
    yb_rows = pl.BlockSpec((tm, GROUP_W), lambda i: (i % tiles_per_seq, i // tiles_per_seq))
    yt_rows = pl.BlockSpec((tm // ATTN_TILE, GROUP_W, ATTN_TILE), lambda i: (i, 0, 0))
    return pl.pallas_call(
        functools.partial(_out_ffn_kernel, final=final),
        out_shape=jax.ShapeDtypeStruct((m, D_MODEL), F32),
        grid=(m // tm,),
        in_specs=[rows(D_MODEL), yt_rows, yb_rows, yt_rows, rows(GROUP_W),
                  once(wout), once(g2), once(wg), once(wu), once(wd), once(gf)],
        out_specs=rows(D_MODEL),
        compiler_params=_params("arbitrary"),
        name="out_ffn",
    )(h, ya, yb, yc, yd, wout, g2, wg, wu, wd, gf)


def _pad_in_proj(w):
    z = lambda n: jnp.zeros(w.shape[:-1] + (n,), w.dtype)
    sb_q = w[..., 928:928 + GROUP_W] * (LOG2_E * HEAD_DIM ** -0.5)
    return jnp.concatenate([w[..., :384], z(MLA_NOPE), w[..., 384:416], z(LANES - MLA_NOPE - MLA_ROPE),
                            w[..., 416:928], sb_q, w[..., 928 + GROUP_W:]], axis=-1).astype(BF16)


def _pad_heads(w, width):
    lead = w.shape[:-1]
    w = w.reshape(lead + (N_HEADS, width))
    w = jnp.pad(w, ((0, 0),) * (len(lead) + 1) + ((0, LANES - width),))
    return w.reshape(lead + (N_HEADS * LANES,))


def _block_diag(w):
    n, c, d = w.shape[-3:]
    eye = jnp.eye(n, dtype=w.dtype)
    return (eye[:, None, :, None] * w[..., :, :, None, :]).reshape(w.shape[:-3] + (n * c, n * d))


def kernel(x, positions, ln1_g, w_in, mla_q_norm_g, mla_w_uq, mla_kv_norm_g, mla_w_ukv, lru_conv_w, lru_conv_b, lru_w_a, lru_b_a, lru_w_x, lru_b_x, lru_lambda, hgrn_lb_logits, hgrn_norm_g, group_norm_g, w_out, ln2_g, w_ffn_gate, w_ffn_up, w_ffn_down, final_norm_g):
    batch, seq, d_model = x.shape
    depth = w_in.shape[0]
    m = batch * seq
    row = lambda v: v.reshape(1, -1)
    h = x.reshape(m, d_model)
    tabs = _rope_tables(positions)
    qk_scale = LOG2_E * (MLA_NOPE + MLA_ROPE) ** -0.5

    win = _pad_in_proj(w_in)
    wuq = _pad_heads(mla_w_uq * qk_scale, MLA_NOPE + MLA_ROPE).astype(BF16)
    wukv = mla_w_ukv.reshape(depth, MLA_KV_LORA, N_HEADS, MLA_NOPE + MLA_V)
    wuk = _pad_heads(wukv[..., :MLA_NOPE].reshape(depth, MLA_KV_LORA, -1), MLA_NOPE)
    wuv = wukv[..., MLA_NOPE:].reshape(depth, MLA_KV_LORA, -1)
    wukv = jnp.concatenate([wuk, wuv], axis=-1).astype(BF16)
    wab = jnp.concatenate([_block_diag(lru_w_a), _block_diag(lru_w_x)], axis=-1).astype(BF16)
    wout, wg, wu, wd = (w.astype(BF16) for w in (w_out, w_ffn_gate, w_ffn_up, w_ffn_down))

    for l in range(depth):
        qm, km, vm_t, lru_x, lru_gate, sb_qk, sb_vt, hg_in = _in_proj(
            h, row(ln1_g[l]), win, row(mla_q_norm_g[l]), wuq, row(mla_kv_norm_g[l]), wukv, tabs, l, batch)

        y_a = _mla_attention(qm, km, vm_t, group_norm_g[l, 0].reshape(-1, 1), batch)
        y_b = _rglru(lru_x, lru_gate, lru_conv_w[l], row(lru_conv_b[l]), wab[l], row(lru_b_a[l]),
                     row(lru_b_x[l]), row(lru_lambda[l]), row(group_norm_g[l, 1]))

        y_c = _sb_attention(sb_qk, sb_vt, group_norm_g[l, 2].reshape(-1, 1), batch)
        y_d = _hgrn2(hg_in, hgrn_lb_logits, row(hgrn_norm_g[l]), l, batch)

        h = _out_ffn(h, y_a, y_b, y_c, y_d, wout, row(ln2_g[l]), wg, wu, wd, row(final_norm_g),
                     l, final=(l == depth - 1))
    return h.reshape(batch, seq, d_model)
```
